```python
import jax, jax.numpy as jnp
from jax import lax
import numpy as np

D_MODEL = 2048
BATCH = 4
SEQ = 4096
DEPTH = 2

CTX_LEN = 256
GRID_W = 64
EPS = 1e-6
HEAD_DIM = 128
ROT_AXIS = HEAD_DIM // 2
ROPE_THETA = 10000.0
Q_BLOCK = 128
ATTN_WIDTH = D_MODEL // 2
N_Q_HEADS = ATTN_WIDTH // HEAD_DIM
N_KV_HEADS = N_Q_HEADS // 4
KV_WIDTH = N_KV_HEADS * HEAD_DIM
SCONV_WIDTH = D_MODEL // 2
SCONV_K = 3
SCONV_PAD = 1
EVEN_SIZES = (ATTN_WIDTH, KV_WIDTH, KV_WIDTH, SCONV_WIDTH, SCONV_WIDTH, SCONV_WIDTH)
EVEN_IN = sum(EVEN_SIZES)
EVEN_OUT = ATTN_WIDTH + SCONV_WIDTH
LRU_WIDTH = 5 * D_MODEL // 4
LRU_BLOCK = 256
N_LRU_BLOCKS = LRU_WIDTH // LRU_BLOCK
LRU_CONV_K = 4
LRU_CONV_PAD = 1
LRU_C = 8.0
N_EXPERTS = 16
EXPERT_FF = D_MODEL
EC_FACTOR = 2
N_EVEN = (DEPTH + 1) // 2
N_ODD = DEPTH // 2

kernel_name = "hybrid_attn_conv_rglru_ecmoe_dit"


def rmsnorm(x, g):
    xf = x.astype(jnp.float32)
    y = xf * lax.rsqrt(jnp.mean(xf * xf, axis=-1, keepdims=True) + EPS)
    return (y * g.astype(jnp.float32)).astype(x.dtype)


def adaln(cvec, w, b):
    m = jax.nn.silu(cvec) @ w + b
    return jnp.split(m, 6, axis=-1)


def modulate(h, shift, scale):
    return h * (1 + scale) + shift


def split_cols(p, sizes):
    idx = np.cumsum(np.array(sizes))[:-1].tolist()
    return jnp.split(p, idx, axis=-1)


def heads(t, nh):
    return t.reshape(t.shape[:2] + (nh, HEAD_DIM))


def headnorm(t, g):
    tf = t.astype(jnp.float32)
    y = tf * lax.rsqrt(jnp.mean(tf * tf, axis=-1, keepdims=True) + EPS)
    return (y * g.astype(jnp.float32)).astype(t.dtype)


def rope_tables(pos):
    freqs = ROPE_THETA ** (-jnp.arange(0, ROT_AXIS, 2, dtype=jnp.float32) / ROT_AXIS)
    ang = pos.astype(jnp.float32)[:, None] * freqs[None, :]
    return jnp.cos(ang)[None, :, None, :], jnp.sin(ang)[None, :, None, :]


def rotate(u, cos, sin):
    u1, u2 = jnp.split(u, 2, axis=-1)
    return jnp.concatenate([u1 * cos - u2 * sin, u1 * sin + u2 * cos], axis=-1)


def rope2d(t, rope):
    cos_r, sin_r, cos_c, sin_c = rope
    tf = t.astype(jnp.float32)
    out = jnp.concatenate([rotate(tf[..., :ROT_AXIS], cos_r, sin_r),
                           rotate(tf[..., ROT_AXIS:], cos_c, sin_c)], axis=-1)
    return out.astype(t.dtype)


def block_attention(q, k, v):
    bsz, n = q.shape[0], q.shape[1]
    grp = N_Q_HEADS // N_KV_HEADS
    qb = q.reshape(bsz, n // Q_BLOCK, Q_BLOCK, N_KV_HEADS, grp, HEAD_DIM).transpose(1, 0, 2, 3, 4, 5)
    scale = HEAD_DIM ** -0.5

    def one(qblk):
        s = jnp.einsum('bqkgd,bskd->bkgqs', qblk, k).astype(jnp.float32) * scale
        p = jax.nn.softmax(s, axis=-1).astype(v.dtype)
        return jnp.einsum('bkgqs,bskd->bqkgd', p, v)

    o = lax.map(one, qb)
    return o.transpose(1, 0, 2, 3, 4, 5).reshape(bsz, n, N_Q_HEADS * HEAD_DIM)


def dwconv(u, w, b, pad_left):
    k_w = w.shape[0]
    n = u.shape[1]
    up = jnp.pad(u, ((0, 0), (pad_left, k_w - 1 - pad_left), (0, 0)))
    acc = up[:, 0:n] * w[0]
    for j in range(1, k_w):
        acc = acc + up[:, j:j + n] * w[j]
    return acc + b


def even_mixer(a_lat, a_ctx, w_in, q_g, k_g, conv_w, conv_b, w_out, rope, need_ctx):
    ql, kl, vl, bl, cl, ul = split_cols(a_lat @ w_in, EVEN_SIZES)
    ql = rope2d(headnorm(heads(ql, N_Q_HEADS), q_g), rope)
    kl = rope2d(headnorm(heads(kl, N_KV_HEADS), k_g), rope)
    vl = heads(vl, N_KV_HEADS)
    if need_ctx:
        qc, kc, vc, bc, cc, uc = split_cols(a_ctx @ w_in, EVEN_SIZES)
    else:
        kc, vc = split_cols(a_ctx @ w_in[:, ATTN_WIDTH:ATTN_WIDTH + 2 * KV_WIDTH], (KV_WIDTH, KV_WIDTH))
    kc = headnorm(heads(kc, N_KV_HEADS), k_g)
    vc = heads(vc, N_KV_HEADS)
    k_all = jnp.concatenate([kc, kl], axis=1)
    v_all = jnp.concatenate([vc, vl], axis=1)
    att_lat = block_attention(ql, k_all, v_all)
    conv_lat = bl * dwconv(cl * ul, conv_w, conv_b, SCONV_PAD)
    y_lat = jnp.concatenate([att_lat, conv_lat], axis=-1) @ w_out
    if not need_ctx:
        return y_lat, None
    qc = headnorm(heads(qc, N_Q_HEADS), q_g)
    att_ctx = block_attention(qc, kc, vc)
    conv_ctx = bc * dwconv(cc * uc, conv_w, conv_b, SCONV_PAD)
    y_ctx = jnp.concatenate([att_ctx, conv_ctx], axis=-1) @ w_out
    return y_lat, y_ctx


def blockdiag(u, w):
    bsz, n, width = u.shape
    ub = u.reshape(bsz, n, N_LRU_BLOCKS, LRU_BLOCK)
    return jnp.einsum('bnhi,hij->bnhj', ub, w).reshape(bsz, n, width)


def rglru_coeffs(u, wa, ba, wx, bx, lam):
    r = jax.nn.sigmoid((blockdiag(u, wa) + ba).astype(jnp.float32))
    ig = jax.nn.sigmoid((blockdiag(u, wx) + bx).astype(jnp.float32))
    log_a = -LRU_C * r * jax.nn.softplus(-lam.astype(jnp.float32))
    a = jnp.exp(log_a)
    bterm = jnp.sqrt(-jnp.expm1(2.0 * log_a)) * ig * u.astype(jnp.float32)
    return a, bterm


def linear_scan(a, b, h0, reverse):
    def step(h, ab):
        a_t, b_t = ab
        h = a_t * h + b_t
        return h, h
    h_fin, ys = lax.scan(step, h0, (a.swapaxes(0, 1), b.swapaxes(0, 1)), reverse=reverse)
    return ys.swapaxes(0, 1), h_fin


def odd_mixer(a_lat, a_ctx, w_in, conv_w, conv_b, wa, ba, wx, bx, lam, w_out, need_ctx):
    xl, gl = split_cols(a_lat @ w_in, (LRU_WIDTH, LRU_WIDTH))
    if need_ctx:
        xc, gc = split_cols(a_ctx @ w_in, (LRU_WIDTH, LRU_WIDTH))
    else:
        xc = a_ctx @ w_in[:, :LRU_WIDTH]
    ul = dwconv(xl, conv_w, conv_b, LRU_CONV_PAD)
    uc = dwconv(xc, conv_w, conv_b, LRU_CONV_PAD)
    h0 = jnp.zeros((a_ctx.shape[0], LRU_WIDTH), jnp.float32)
    y_lat = jnp.zeros(ul.shape, jnp.float32)
    y_ctx = jnp.zeros(uc.shape, jnp.float32)
    for d, rev in ((0, False), (1, True)):
        ac, bc = rglru_coeffs(uc, wa[d], ba[d], wx[d], bx[d], lam[d])
        hc, hc_fin = linear_scan(ac, bc, h0, rev)
        al, bl = rglru_coeffs(ul, wa[d], ba[d], wx[d], bx[d], lam[d])
        hl, _ = linear_scan(al, bl, hc_fin, rev)
        y_lat = y_lat + hl
        y_ctx = y_ctx + hc
    out_lat = (y_lat.astype(a_lat.dtype) * jax.nn.gelu(gl)) @ w_out
    if not need_ctx:
        return out_lat, None
    out_ctx = (y_ctx.astype(a_ctx.dtype) * jax.nn.gelu(gc)) @ w_out
    return out_lat, out_ctx


def ec_moe(h, w_router, w_gate, w_up, w_down):
    n = h.shape[1]
    cap = max(1, EC_FACTOR * n // N_EXPERTS)
    logits = jnp.einsum('bnd,de->bne', h, w_router).astype(jnp.float32)
    aff = jax.nn.softmax(logits, axis=-1)
    g, idx = lax.top_k(aff.transpose(0, 2, 1), cap)
    xs = jax.vmap(lambda hb, ib: hb[ib])(h, idx)
    hid = jax.nn.silu(jnp.einsum('becd,edf->becf', xs, w_gate)) * jnp.einsum('becd,edf->becf', xs, w_up)
    out = jnp.einsum('becf,efd->becd', hid, w_down) * g[..., None].astype(h.dtype)
    d = h.shape[-1]
    return jax.vmap(lambda ob, ib: jnp.zeros((n, d), ob.dtype).at[ib.reshape(-1)].add(ob.reshape(-1, d)))(out, idx)


def setup_inputs(seed: int = 0) -> dict:
    key = jax.random.key(seed)
    ks = jax.random.split(key, 32)
    f32 = jnp.float32
    D = D_MODEL

    def nrm(k, shape, scale):
        return jax.random.normal(k, shape, f32) * scale

    u = jax.random.uniform(ks[20], (N_ODD, 2, LRU_WIDTH), f32, 0.9, 0.999)
    a_base = u ** (1.0 / LRU_C)
    od_lam = jnp.log(a_base) - jnp.log1p(-a_base)
    return {
        "x": nrm(ks[0], (BATCH, SEQ, D), 1.0),
        "c": nrm(ks[1], (BATCH, D), 1.0),
        "ctx": nrm(ks[2], (BATCH, CTX_LEN, D), 1.0),
        "c_ctx": nrm(ks[3], (D,), 1.0),
        "mod_w": nrm(ks[4], (DEPTH, D, 6 * D), D ** -0.5),
        "mod_b": nrm(ks[5], (DEPTH, 6 * D), 0.01),
        "norm1_g": 1.0 + nrm(ks[6], (DEPTH, D), 0.02),
        "norm2_g": 1.0 + nrm(ks[7], (DEPTH, D), 0.02),
        "ev_w_in": nrm(ks[8], (N_EVEN, D, EVEN_IN), D ** -0.5),
        "ev_q_norm": 1.0 + nrm(ks[9], (N_EVEN, HEAD_DIM), 0.02),
        "ev_k_norm": 1.0 + nrm(ks[10], (N_EVEN, HEAD_DIM), 0.02),
        "ev_conv_w": nrm(ks[11], (N_EVEN, SCONV_K, SCONV_WIDTH), SCONV_K ** -0.5),
        "ev_conv_b": nrm(ks[12], (N_EVEN, SCONV_WIDTH), 0.01),
        "ev_w_out": nrm(ks[13], (N_EVEN, EVEN_OUT, D), EVEN_OUT ** -0.5),
        "od_w_in": nrm(ks[14], (N_ODD, D, 2 * LRU_WIDTH), D ** -0.5),
        "od_conv_w": nrm(ks[15], (N_ODD, LRU_CONV_K, LRU_WIDTH), LRU_CONV_K ** -0.5),
        "od_conv_b": nrm(ks[16], (N_ODD, LRU_WIDTH), 0.01),
        "od_wa": nrm(ks[17], (N_ODD, 2, N_LRU_BLOCKS, LRU_BLOCK, LRU_BLOCK), LRU_BLOCK ** -0.5),
        "od_ba": nrm(ks[18], (N_ODD, 2, LRU_WIDTH), 0.01),
        "od_wx": nrm(ks[19], (N_ODD, 2, N_LRU_BLOCKS, LRU_BLOCK, LRU_BLOCK), LRU_BLOCK ** -0.5),
        "od_bx": nrm(ks[21], (N_ODD, 2, LRU_WIDTH), 0.01),
        "od_lam": od_lam,
        "od_w_out": nrm(ks[22], (N_ODD, LRU_WIDTH, D), LRU_WIDTH ** -0.5),
        "moe_router": nrm(ks[23], (DEPTH, D, N_EXPERTS), D ** -0.5),
        "moe_w_gate": nrm(ks[24], (DEPTH, N_EXPERTS, D, EXPERT_FF), D ** -0.5),
        "moe_w_up": nrm(ks[25], (DEPTH, N_EXPERTS, D, EXPERT_FF), D ** -0.5),
        "moe_w_down": nrm(ks[26], (DEPTH, N_EXPERTS, EXPERT_FF, D), EXPERT_FF ** -0.5),
        "final_norm_g": 1.0 + nrm(ks[27], (D,), 0.02),
    }


def reference(x, c, ctx, c_ctx, mod_w, mod_b, norm1_g, norm2_g, ev_w_in, ev_q_norm, ev_k_norm,
              ev_conv_w, ev_conv_b, ev_w_out, od_w_in, od_conv_w, od_conv_b, od_wa, od_ba, od_wx,
              od_bx, od_lam, od_w_out, moe_router, moe_w_gate, moe_w_up, moe_w_down, final_norm_g):
    n = x.shape[1]
    rows = n // GRID_W
    row = jnp.repeat(jnp.arange(rows), GRID_W)
    col = jnp.tile(jnp.arange(GRID_W), rows)
    cos_r, sin_r = rope_tables(row)
    cos_c, sin_c = rope_tables(col)
    rope = (cos_r, sin_r, cos_c, sin_c)

    h_lat = x
    h_ctx = ctx
    for i in range(DEPTH):
        need_ctx = i < DEPTH - 1
        ml = [m[:, None, :] for m in adaln(c, mod_w[i], mod_b[i])]
        mc = adaln(c_ctx, mod_w[i], mod_b[i])
        a_lat = modulate(rmsnorm(h_lat, norm1_g[i]), ml[0], ml[1])
        a_ctx = modulate(rmsnorm(h_ctx, norm1_g[i]), mc[0], mc[1])
        if i % 2 == 0:
            j = i // 2
            y_lat, y_ctx = even_mixer(a_lat, a_ctx, ev_w_in[j], ev_q_norm[j], ev_k_norm[j],
                                      ev_conv_w[j], ev_conv_b[j], ev_w_out[j], rope, need_ctx)
        else:
            j = i // 2
            y_lat, y_ctx = odd_mixer(a_lat, a_ctx, od_w_in[j], od_conv_w[j], od_conv_b[j], od_wa[j],
                                     od_ba[j], od_wx[j], od_bx[j], od_lam[j], od_w_out[j], need_ctx)
        h_lat = h_lat + ml[2] * y_lat
        b_lat = modulate(rmsnorm(h_lat, norm2_g[i]), ml[3], ml[4])
        h_lat = h_lat + ml[5] * ec_moe(b_lat, moe_router[i], moe_w_gate[i], moe_w_up[i], moe_w_down[i])
        if need_ctx:
            h_ctx = h_ctx + mc[2] * y_ctx
            b_ctx = modulate(rmsnorm(h_ctx, norm2_g[i]), mc[3], mc[4])
            h_ctx = h_ctx + mc[5] * ec_moe(b_ctx, moe_router[i], moe_w_gate[i], moe_w_up[i], moe_w_down[i])
    return rmsnorm(h_lat, final_norm_g)
```

```python
import functools

import jax
import jax.numpy as jnp
import numpy as np
from jax import lax
from jax.experimental import pallas as pl
from jax.experimental.pallas import tpu as pltpu

F32 = jnp.float32
BF16 = jnp.bfloat16
I32 = jnp.int32

EPS = 1e-6
HEAD_DIM = 128
ROT_AXIS = HEAD_DIM // 2
ROPE_THETA = 10000.0
GRID_W = 64
Q_PER_KV = 4
LRU_BLOCK = 256
LRU_C = 8.0
N_EXPERTS = 16
EC_FACTOR = 2

LANES = 128
SUBLANES = 8
ROW_TILE = 256
KV_CHUNK = 512
MIB = 1024 * 1024


def _cparams(sem, vmem_mib):
    return pltpu.CompilerParams(dimension_semantics=sem, vmem_limit_bytes=int(vmem_mib * MIB))


def _sigmoid(x):
    return 1.0 / (1.0 + jnp.exp(-x))


def _rms(x, g):
    return x * lax.rsqrt(jnp.mean(x * x, axis=-1, keepdims=True) + EPS) * g


def _dot(a, b):
    return jnp.dot(a, b, preferred_element_type=F32)


def _dot_nt(a, b):
    return lax.dot_general(a, b, (((1,), (1,)), ((), ())), preferred_element_type=F32)


def _resident(shape):
    nd = len(shape)
    return pl.BlockSpec(shape, lambda *_: (0,) * nd)


def _mods_kernel(c_ref, w_ref, b_ref, o_ref):
    c = c_ref[...]
    s = (c * _sigmoid(c)).astype(BF16)
    o_ref[...] = _dot(s, w_ref[...].astype(BF16)) + b_ref[...]


def _mods(cvec, mod_w, mod_b):
    depth, d, n6 = mod_w.shape
    tn = 1536
    return pl.pallas_call(
        _mods_kernel,
        grid=(depth, n6 // tn),
        in_specs=[
            pl.BlockSpec((SUBLANES, d), lambda l, j: (0, 0)),
            pl.BlockSpec((None, d, tn), lambda l, j: (l, 0, j)),
            pl.BlockSpec((None, 1, tn), lambda l, j: (l, 0, j)),
        ],
        out_specs=pl.BlockSpec((None, SUBLANES, tn), lambda l, j: (l, 0, j)),
        out_shape=jax.ShapeDtypeStruct((depth, SUBLANES, n6), F32),
        compiler_params=_cparams(("arbitrary", "arbitrary"), 40),
        name="adaln_mods",
    )(cvec, mod_w, mod_b.reshape(depth, 1, n6))


def _mod_spec(bsz, n_ctx_tiles, d, k):
    return pl.BlockSpec((None, 1, d), lambda b, i: (jnp.where(i < n_ctx_tiles, bsz, b), 0, k))


def _even_in_kernel(h_ref, g1_ref, sh_ref, sc_ref, w_ref, qg_ref, kg_ref, cos_ref, sin_ref,
                    q_ref, k_ref, v_ref, gb_ref, p_ref, *, attn_w, kv_w, conv_w):
    x = h_ref[...]
    a = (_rms(x, g1_ref[...]) * (1.0 + sc_ref[...]) + sh_ref[...]).astype(BF16)
    cos = cos_ref[...]
    sin = sin_ref[...]
    tm = x.shape[0]
    lane = lax.broadcasted_iota(I32, (tm, HEAD_DIM), 1)
    low_half = (lane % ROT_AXIS) < (ROT_AXIS // 2)

    def norm_rope(t, g, scale):
        t = t * lax.rsqrt(jnp.mean(t * t, axis=-1, keepdims=True) + EPS) * g
        swapped = jnp.where(low_half, pltpu.roll(t, HEAD_DIM - ROT_AXIS // 2, 1),
                            pltpu.roll(t, ROT_AXIS // 2, 1))
        return (t * cos + swapped * sin) * scale

    cw = 4 * HEAD_DIM
    qg = qg_ref[...]
    kg = kg_ref[...]
    q_scale = HEAD_DIM ** -0.5
    for c0 in range(0, attn_w, cw):
        y = _dot(a, w_ref[:, c0:c0 + cw])
        for j in range(cw // HEAD_DIM):
            t = norm_rope(y[:, j * HEAD_DIM:(j + 1) * HEAD_DIM], qg, q_scale)
            q_ref[:, c0 + j * HEAD_DIM:c0 + (j + 1) * HEAD_DIM] = t.astype(BF16)
    y = _dot(a, w_ref[:, attn_w:attn_w + 2 * kv_w])
    for j in range(kv_w // HEAD_DIM):
        t = norm_rope(y[:, j * HEAD_DIM:(j + 1) * HEAD_DIM], kg, 1.0)
        k_ref[:, j * HEAD_DIM:(j + 1) * HEAD_DIM] = t.astype(BF16)
    v_ref[...] = y[:, kv_w:2 * kv_w].astype(BF16)
    off_b = attn_w + 2 * kv_w
    off_c = off_b + conv_w
    off_u = off_c + conv_w
    for c0 in range(0, conv_w, cw):
        gb_ref[:, c0:c0 + cw] = _dot(a, w_ref[:, off_b + c0:off_b + c0 + cw]).astype(BF16)
        cc = _dot(a, w_ref[:, off_c + c0:off_c + c0 + cw])
        uu = _dot(a, w_ref[:, off_u + c0:off_u + c0 + cw])
        p_ref[:, c0:c0 + cw] = (cc * uu).astype(BF16)


def _even_in(h, mods3, g1, w_in, qg, kg, cos_t, sin_t, ctx_len, attn_w, kv_w, conv_w):
    bsz, L, d = h.shape
    tm = ROW_TILE
    nct = ctx_len // tm
    tok = lambda width: pl.BlockSpec((None, tm, width), lambda b, i: (b, i, 0))
    kern = functools.partial(_even_in_kernel, attn_w=attn_w, kv_w=kv_w, conv_w=conv_w)
    return pl.pallas_call(
        kern,
        grid=(bsz, L // tm),
        in_specs=[
            tok(d),
            _resident((1, d)),
            _mod_spec(bsz, nct, d, 0),
            _mod_spec(bsz, nct, d, 1),
            _resident(w_in.shape),
            _resident((1, HEAD_DIM)),
            _resident((1, HEAD_DIM)),
            pl.BlockSpec((tm, HEAD_DIM), lambda b, i: (i, 0)),
            pl.BlockSpec((tm, HEAD_DIM), lambda b, i: (i, 0)),
        ],
        out_specs=[tok(attn_w), tok(kv_w), tok(kv_w), tok(conv_w), tok(conv_w)],
        out_shape=[
            jax.ShapeDtypeStruct((bsz, L, attn_w), BF16),
            jax.ShapeDtypeStruct((bsz, L, kv_w), BF16),
            jax.ShapeDtypeStruct((bsz, L, kv_w), BF16),
            jax.ShapeDtypeStruct((bsz, L, conv_w), BF16),
            jax.ShapeDtypeStruct((bsz, L, conv_w), BF16),
        ],
        compiler_params=_cparams(("arbitrary", "arbitrary"), 56),
        name="even_in_proj",
    )(h, g1, mods3, mods3, w_in, qg, kg, cos_t, sin_t)


def _attn_kernel(q_ref, k_ref, v_ref, o_ref, m_sc, l_sc, acc_sc, *, ctx_len, n_lat_chunks, tq):
    qi = pl.program_id(2)
    q = q_ref[...]
    qs = jnp.concatenate([q[:, j * HEAD_DIM:(j + 1) * HEAD_DIM] for j in range(Q_PER_KV)], axis=0)

    s = _dot_nt(qs, k_ref[0:ctx_len, :])
    m0 = jnp.max(s, axis=-1, keepdims=True)
    p = jnp.exp(s - m0)
    m_sc[...] = m0
    l_sc[...] = jnp.sum(p, axis=-1, keepdims=True)
    acc_sc[...] = _dot(p.astype(BF16), v_ref[0:ctx_len, :])

    @pl.when(qi >= ctx_len // tq)
    def _latent_keys():
        def body(c, carry):
            start = pl.multiple_of(ctx_len + c * KV_CHUNK, KV_CHUNK // 2)
            kk = k_ref[pl.ds(start, KV_CHUNK), :]
            vv = v_ref[pl.ds(start, KV_CHUNK), :]
            sc = _dot_nt(qs, kk)
            m_old = m_sc[...]
            m_new = jnp.maximum(m_old, jnp.max(sc, axis=-1, keepdims=True))
            alpha = jnp.exp(m_old - m_new)
            pc = jnp.exp(sc - m_new)
            l_sc[...] = alpha * l_sc[...] + jnp.sum(pc, axis=-1, keepdims=True)
            acc_sc[...] = alpha * acc_sc[...] + _dot(pc.astype(BF16), vv)
            m_sc[...] = m_new
            return carry
        lax.fori_loop(0, n_lat_chunks, body, 0)

    o = acc_sc[...] / l_sc[...]
    for j in range(Q_PER_KV):
        o_ref[:, j * HEAD_DIM:(j + 1) * HEAD_DIM] = o[j * tq:(j + 1) * tq, :].astype(BF16)


def _attention(q, k, v, ctx_len):
    bsz, L, attn_w = q.shape
    n_kv = k.shape[2] // HEAD_DIM
    tq = ROW_TILE
    gw = Q_PER_KV * HEAD_DIM
    seq = L - ctx_len
    kern = functools.partial(_attn_kernel, ctx_len=ctx_len, n_lat_chunks=seq // KV_CHUNK, tq=tq)
    return pl.pallas_call(
        kern,
        grid=(bsz, n_kv, L // tq),
        in_specs=[
            pl.BlockSpec((None, tq, gw), lambda b, g, i: (b, i, g)),
            pl.BlockSpec((None, L, HEAD_DIM), lambda b, g, i: (b, 0, g)),
            pl.BlockSpec((None, L, HEAD_DIM), lambda b, g, i: (b, 0, g)),
        ],
        out_specs=pl.BlockSpec((None, tq, gw), lambda b, g, i: (b, i, g)),
        out_shape=jax.ShapeDtypeStruct((bsz, L, attn_w), BF16),
        scratch_shapes=[
            pltpu.VMEM((Q_PER_KV * tq, 1), F32),
            pltpu.VMEM((Q_PER_KV * tq, 1), F32),
            pltpu.VMEM((Q_PER_KV * tq, HEAD_DIM), F32),
        ],
        compiler_params=_cparams(("arbitrary", "arbitrary", "arbitrary"), 40),
        name="attention",
    )(q, k, v)


def _gelu_tanh(x):
    return 0.5 * x * (1.0 + jnp.tanh(np.sqrt(2.0 / np.pi) * (x + 0.044715 * (x * x * x))))


def _odd_in_kernel(h_ref, g1_ref, sh_ref, sc_ref, w_ref, x_ref, gg_ref, *, lru_w):
    x = h_ref[...]
    a = (_rms(x, g1_ref[...]) * (1.0 + sc_ref[...]) + sh_ref[...]).astype(BF16)
    cw = 512
    for c0 in range(0, lru_w, cw):
        x_ref[:, c0:c0 + cw] = _dot(a, w_ref[:, c0:c0 + cw])
        gg_ref[:, c0:c0 + cw] = _gelu_tanh(_dot(a, w_ref[:, lru_w + c0:lru_w + c0 + cw])).astype(BF16)


def _odd_in(h, mods3, g1, w_in, ctx_len, lru_w):
    bsz, L, d = h.shape
    tm = ROW_TILE
    nct = ctx_len // tm
    tok = lambda width: pl.BlockSpec((None, tm, width), lambda b, i: (b, i, 0))
    return pl.pallas_call(
        functools.partial(_odd_in_kernel, lru_w=lru_w),
        grid=(bsz, L // tm),
        in_specs=[tok(d), _resident((1, d)), _mod_spec(bsz, nct, d, 0), _mod_spec(bsz, nct, d, 1),
                  _resident(w_in.shape)],
        out_specs=[tok(lru_w), tok(lru_w)],
        out_shape=[jax.ShapeDtypeStruct((bsz, L, lru_w), F32),
                   jax.ShapeDtypeStruct((bsz, L, lru_w), BF16)],
        compiler_params=_cparams(("arbitrary", "arbitrary"), 56),
        name="odd_in_proj",
    )(h, g1, mods3, mods3, w_in)


def _lru_tile(d, j, n_tiles, n_ctx_tiles):
    rev = jnp.where(j < n_ctx_tiles, n_ctx_tiles - 1 - j, n_tiles - 1 - (j - n_ctx_tiles))
    return jnp.where(d == 0, j, rev)


def _lru_kernel(x_ref, xp_ref, xn_ref, cw_ref, cb_ref, wa_ref, wx_ref, ba_ref, bx_ref, lam_ref,
                y_ref, a_sc, b_sc, y_sc, h_sc, *, n_tiles, n_ctx_tiles, tt):
    d = pl.program_id(1)
    j = pl.program_id(2)
    tile = _lru_tile(d, j, n_tiles, n_ctx_tiles)
    width = x_ref.shape[1]
    x = x_ref[...]
    seg_first = jnp.logical_or(tile == 0, tile == n_ctx_tiles)
    seg_last = jnp.logical_or(tile == n_ctx_tiles - 1, tile == n_tiles - 1)
    prev = jnp.where(seg_first, 0.0, xp_ref[SUBLANES - 1:SUBLANES, :])
    nxt1 = jnp.where(seg_last, 0.0, xn_ref[0:1, :])
    nxt2 = jnp.where(seg_last, 0.0, xn_ref[1:2, :])
    row = lax.broadcasted_iota(I32, (tt, 1), 0)
    x_m1 = jnp.where(row == 0, prev, pltpu.roll(x, 1, 0))
    x_p1 = jnp.where(row == tt - 1, nxt1, pltpu.roll(x, tt - 1, 0))
    x_p2 = jnp.where(row == tt - 1, nxt2, jnp.where(row == tt - 2, nxt1, pltpu.roll(x, tt - 2, 0)))
    u = (x_m1 * cw_ref[0:1, :] + x * cw_ref[1:2, :] + x_p1 * cw_ref[2:3, :] + x_p2 * cw_ref[3:4, :]
         + cb_ref[...])
    z = -lam_ref[...]
    c8 = LRU_C * (jnp.maximum(z, 0.0) + jnp.log(1.0 + jnp.exp(-jnp.abs(z))))
    for blk in range(width // LRU_BLOCK):
        cs = slice(blk * LRU_BLOCK, (blk + 1) * LRU_BLOCK)
        ub = u[:, cs]
        ub16 = ub.astype(BF16)
        r = _sigmoid(_dot(ub16, wa_ref[blk]) + ba_ref[:, cs])
        ig = _sigmoid(_dot(ub16, wx_ref[blk]) + bx_ref[:, cs])
        a = jnp.exp(-r * c8[:, cs])
        a_sc[:, cs] = a
        b_sc[:, cs] = jnp.sqrt(1.0 - a * a) * ig * ub

    @pl.when(j == 0)
    def _reset():
        h_sc[...] = jnp.zeros_like(h_sc)

    sub = lax.broadcasted_iota(I32, (SUBLANES, width), 0)
    n_groups = tt // SUBLANES

    def group_scan(base, forward):
        av = a_sc[pl.ds(base, SUBLANES), :]
        bv = b_sc[pl.ds(base, SUBLANES), :]
        for s in (1, 2, 4):
            shift = s if forward else SUBLANES - s
            keep = (sub >= s) if forward else (sub < SUBLANES - s)
            a_sh = pltpu.roll(av, shift, 0)
            b_sh = pltpu.roll(bv, shift, 0)
            bv = jnp.where(keep, av * b_sh + bv, bv)
            av = jnp.where(keep, av * a_sh, av)
        out = av * h_sc[...] + bv
        y_sc[pl.ds(base, SUBLANES), :] = out
        last = out[SUBLANES - 1:SUBLANES, :] if forward else out[0:1, :]
        h_sc[...] = jnp.broadcast_to(last, (SUBLANES, width))

    @pl.when(d == 0)
    def _forward():
        def body(g, carry):
            group_scan(pl.multiple_of(g * SUBLANES, SUBLANES), True)
            return carry
        lax.fori_loop(0, n_groups, body, 0)

    @pl.when(d == 1)
    def _backward():
        def body(g, carry):
            group_scan(pl.multiple_of((n_groups - 1 - g) * SUBLANES, SUBLANES), False)
            return carry
        lax.fori_loop(0, n_groups, body, 0)

    y_ref[...] = y_sc[...].astype(BF16)


def _lru(xpre, conv_w, conv_b, wa, wx, ba, bx, lam, ctx_len):
    bsz, L, width = xpre.shape
    tt = ROW_TILE
    nt = L // tt
    nct = ctx_len // tt
    nb = width // LRU_BLOCK
    hb = tt // SUBLANES
    tile = lambda d, j: _lru_tile(d, j, nt, nct)
    kern = functools.partial(_lru_kernel, n_tiles=nt, n_ctx_tiles=nct, tt=tt)
    dir_vec = pl.BlockSpec((None, 1, width), lambda b, d, j: (d, 0, 0))
    dir_mat = pl.BlockSpec((None, nb, LRU_BLOCK, LRU_BLOCK), lambda b, d, j: (d, 0, 0, 0))
    return pl.pallas_call(
        kern,
        grid=(bsz, 2, nt),
        in_specs=[
            pl.BlockSpec((None, tt, width), lambda b, d, j: (b, tile(d, j), 0)),
            pl.BlockSpec((None, SUBLANES, width),
                         lambda b, d, j: (b, jnp.maximum(tile(d, j) * hb - 1, 0), 0)),
            pl.BlockSpec((None, SUBLANES, width),
                         lambda b, d, j: (b, jnp.minimum((tile(d, j) + 1) * hb, L // SUBLANES - 1), 0)),
            pl.BlockSpec(conv_w.shape, lambda b, d, j: (0, 0)),
            pl.BlockSpec((1, width), lambda b, d, j: (0, 0)),
            dir_mat, dir_mat, dir_vec, dir_vec, dir_vec,
        ],
        out_specs=pl.BlockSpec((None, None, tt, width), lambda b, d, j: (d, b, tile(d, j), 0)),
        out_shape=jax.ShapeDtypeStruct((2, bsz, L, width), BF16),
        scratch_shapes=[pltpu.VMEM((tt, width), F32), pltpu.VMEM((tt, width), F32),
                        pltpu.VMEM((tt, width), F32), pltpu.VMEM((SUBLANES, width), F32)],
        compiler_params=_cparams(("arbitrary", "arbitrary", "arbitrary"), 48),
        name="rglru_scan",
    )(xpre, xpre, xpre, conv_w, conv_b, wa, wx, ba, bx, lam)


def _out_tail(h_ref, y, gate_ref, g2_ref, sh2_ref, sc2_ref, wrh_ref, wrl_ref, ho_ref, bo_ref, aff_ref):
    hn = h_ref[...] + gate_ref[...] * y
    ho_ref[...] = hn
    bn = _rms(hn, g2_ref[...]) * (1.0 + sc2_ref[...]) + sh2_ref[...]
    b_hi = bn.astype(BF16)
    bo_ref[...] = b_hi
    b_lo = (bn - b_hi.astype(F32)).astype(BF16)
    wrh = wrh_ref[...]
    logits = _dot_nt(wrh, b_hi) + _dot_nt(wrh, b_lo) + _dot_nt(wrl_ref[...], b_hi)
    ex = jnp.exp(logits - jnp.max(logits, axis=0, keepdims=True))
    aff_ref[...] = ex / jnp.sum(ex, axis=0, keepdims=True)


def _even_out_kernel(h_ref, att_ref, gb_ref, p_ref, pp_ref, pn_ref, cw_ref, cb_ref, w_ref,
                     gate_ref, g2_ref, sh2_ref, sc2_ref, wrh_ref, wrl_ref,
                     ho_ref, bo_ref, aff_ref, *, n_tiles, n_ctx_tiles, attn_w):
    i = pl.program_id(1)
    tm = p_ref.shape[0]
    p = p_ref[...].astype(F32)
    seg_first = jnp.logical_or(i == 0, i == n_ctx_tiles)
    seg_last = jnp.logical_or(i == n_ctx_tiles - 1, i == n_tiles - 1)
    prev = jnp.where(seg_first, 0.0, pp_ref[...].astype(F32)[2 * SUBLANES - 1:2 * SUBLANES, :])
    nxt = jnp.where(seg_last, 0.0, pn_ref[...].astype(F32)[0:1, :])
    row = lax.broadcasted_iota(I32, (tm, 1), 0)
    p_m1 = jnp.where(row == 0, prev, pltpu.roll(p, 1, 0))
    p_p1 = jnp.where(row == tm - 1, nxt, pltpu.roll(p, tm - 1, 0))
    conv = p_m1 * cw_ref[0:1, :] + p * cw_ref[1:2, :] + p_p1 * cw_ref[2:3, :] + cb_ref[...]
    act = (gb_ref[...].astype(F32) * conv).astype(BF16)
    y = _dot(att_ref[...], w_ref[0:attn_w, :]) + _dot(act, w_ref[attn_w:, :])
    _out_tail(h_ref, y, gate_ref, g2_ref, sh2_ref, sc2_ref, wrh_ref, wrl_ref, ho_ref, bo_ref, aff_ref)


def _odd_out_kernel(h_ref, yf_ref, yr_ref, gg_ref, w_ref,
                    gate_ref, g2_ref, sh2_ref, sc2_ref, wrh_ref, wrl_ref,
                    ho_ref, bo_ref, aff_ref):
    ysum = (yf_ref[...].astype(F32) + yr_ref[...].astype(F32)).astype(BF16)
    act = (ysum.astype(F32) * gg_ref[...].astype(F32)).astype(BF16)
    y = _dot(act, w_ref[...])
    _out_tail(h_ref, y, gate_ref, g2_ref, sh2_ref, sc2_ref, wrh_ref, wrl_ref, ho_ref, bo_ref, aff_ref)


def _out_common(bsz, L, d, ctx_len, n_exp):
    tm = ROW_TILE
    nct = ctx_len // tm
    tail_specs = [
        _mod_spec(bsz, nct, d, 2), _resident((1, d)), _mod_spec(bsz, nct, d, 3), _mod_spec(bsz, nct, d, 4),
        _resident((n_exp, d)), _resident((n_exp, d)),
    ]
    out_specs = [
        pl.BlockSpec((None, tm, d), lambda b, i: (b, i, 0)),
        pl.BlockSpec((None, tm, d), lambda b, i: (b, i, 0)),
        pl.BlockSpec((None, n_exp, tm), lambda b, i: (b, 0, i)),
    ]
    out_shape = [
        jax.ShapeDtypeStruct((bsz, L, d), F32),
        jax.ShapeDtypeStruct((bsz, L, d), BF16),
        jax.ShapeDtypeStruct((bsz, n_exp, L), F32),
    ]
    return tm, nct, tail_specs, out_specs, out_shape


def _even_out(h, att, gb, p, conv_w, conv_b, w_out, mods3, g2, wr_hi, wr_lo, ctx_len):
    bsz, L, d = h.shape
    n_exp = wr_hi.shape[0]
    tm, nct, tail_specs, out_specs, out_shape = _out_common(bsz, L, d, ctx_len, n_exp)
    attn_w = att.shape[2]
    conv_wd = p.shape[2]
    halo = 2 * SUBLANES
    hb = tm // halo
    tok = lambda width: pl.BlockSpec((None, tm, width), lambda b, i: (b, i, 0))
    kern = functools.partial(_even_out_kernel, n_tiles=L // tm, n_ctx_tiles=nct, attn_w=attn_w)
    return pl.pallas_call(
        kern,
        grid=(bsz, L // tm),
        in_specs=[
            tok(d), tok(attn_w), tok(conv_wd), tok(conv_wd),
            pl.BlockSpec((None, halo, conv_wd), lambda b, i: (b, jnp.maximum(i * hb - 1, 0), 0)),
            pl.BlockSpec((None, halo, conv_wd), lambda b, i: (b, jnp.minimum((i + 1) * hb, L // halo - 1), 0)),
            _resident(conv_w.shape), _resident((1, conv_wd)), _resident(w_out.shape),
        ] + tail_specs,
        out_specs=out_specs,
        out_shape=out_shape,
        compiler_params=_cparams(("arbitrary", "arbitrary"), 48),
        name="even_out_proj",
    )(h, att, gb, p, p, p, conv_w, conv_b, w_out, mods3, g2, mods3, mods3, wr_hi, wr_lo)


def _odd_out(h, y2, gg, w_out, mods3, g2, wr_hi, wr_lo, ctx_len):
    bsz, L, d = h.shape
    n_exp = wr_hi.shape[0]
    width = gg.shape[2]
    tm, nct, tail_specs, out_specs, out_shape = _out_common(bsz, L, d, ctx_len, n_exp)
    tok = lambda wd: pl.BlockSpec((None, tm, wd), lambda b, i: (b, i, 0))
    return pl.pallas_call(
        _odd_out_kernel,
        grid=(bsz, L // tm),
        in_specs=[
            tok(d),
            pl.BlockSpec((None, None, tm, width), lambda b, i: (0, b, i, 0)),
            pl.BlockSpec((None, None, tm, width), lambda b, i: (1, b, i, 0)),
            tok(width), _resident(w_out.shape),
        ] + tail_specs,
        out_specs=out_specs,
        out_shape=out_shape,
        compiler_params=_cparams(("arbitrary", "arbitrary"), 48),
        name="odd_out_proj",
    )(h, y2, y2, gg, w_out, mods3, g2, mods3, mods3, wr_hi, wr_lo)


def _route_kernel(aff_ref, key_ref, *, segments):
    n_exp = aff_ref.shape[0]
    for (start, n, cap, slot_off) in segments:
        if cap == 0:
            key_ref[:, start:start + n] = jnp.full((n_exp, n), -1, I32)
            continue
        aff = aff_ref[:, start:start + n]
        bits = pltpu.bitcast(aff, I32)
        capf = jnp.float32(cap)

        def count(mask):
            return jnp.sum(jnp.where(mask, 1.0, 0.0), axis=1, keepdims=True)

        thr = jnp.zeros((n_exp, 1), I32)
        for bit in range(30, -1, -1):
            cand = thr | jnp.int32(1 << bit)
            thr = jnp.where(count(bits >= cand) >= capf, cand, thr)
        gt = bits > thr
        eq = bits == thr
        need = capf - count(gt)

        cwid = 256 if n % 256 == 0 else LANES
        ri = lax.broadcasted_iota(I32, (cwid, cwid), 0)
        ci = lax.broadcasted_iota(I32, (cwid, cwid), 1)
        upper = jnp.where(ri <= ci, 1.0, 0.0).astype(BF16)

        def cumsum_excl(xf):
            outs = []
            carry = jnp.zeros((n_exp, 1), F32)
            for c in range(n // cwid):
                xc = xf[:, c * cwid:(c + 1) * cwid]
                inc = _dot(xc.astype(BF16), upper)
                outs.append(inc - xc + carry)
                carry = carry + inc[:, cwid - 1:cwid]
            return jnp.concatenate(outs, axis=1)

        eqf = jnp.where(eq, 1.0, 0.0)
        sel = jnp.logical_or(gt, jnp.logical_and(eq, cumsum_excl(eqf) < need))
        self_f = jnp.where(sel, 1.0, 0.0)
        slot = cumsum_excl(self_f) + jnp.float32(slot_off)
        key_ref[:, start:start + n] = jnp.where(sel, slot, -1.0).astype(I32)


def _route(aff_t, segments):
    bsz, n_exp, L = aff_t.shape
    return pl.pallas_call(
        functools.partial(_route_kernel, segments=segments),
        grid=(bsz,),
        in_specs=[pl.BlockSpec((None, n_exp, L), lambda b: (b, 0, 0))],
        out_specs=pl.BlockSpec((None, n_exp, L), lambda b: (b, 0, 0)),
        out_shape=jax.ShapeDtypeStruct((bsz, n_exp, L), I32),
        compiler_params=_cparams(("arbitrary",), 32),
        name="moe_route",
    )(aff_t)


def _gather_kernel(key_ref, x_ref, o_ref, *, n_slots):
    e = pl.program_id(1)
    krow = key_ref[pl.ds(e, 1), :]
    slot = lax.broadcasted_iota(I32, (n_slots, 1), 0)
    onehot = jnp.where(krow == slot, 1.0, 0.0).astype(BF16)
    o_ref[...] = _dot(onehot, x_ref[...]).astype(BF16)


def _gather(key, b_act, n_slots):
    bsz, n_exp, L = key.shape
    d = b_act.shape[2]
    return pl.pallas_call(
        functools.partial(_gather_kernel, n_slots=n_slots),
        grid=(bsz, n_exp),
        in_specs=[
            pl.BlockSpec((None, n_exp, L), lambda b, e: (b, 0, 0)),
            pl.BlockSpec((None, L, d), lambda b, e: (b, 0, 0)),
        ],
        out_specs=pl.BlockSpec((None, None, n_slots, d), lambda b, e: (e, b, 0, 0)),
        out_shape=jax.ShapeDtypeStruct((n_exp, bsz, n_slots, d), BF16),
        compiler_params=_cparams(("arbitrary", "arbitrary"), 56),
        name="moe_gather",
    )(key, b_act)


def _expert_kernel(x_ref, wg_ref, wu_ref, wd_ref, o_ref, acc_sc):
    f = pl.program_id(2)
    x = x_ref[...]
    hg = _dot(x, wg_ref[...].astype(BF16))
    hu = _dot(x, wu_ref[...].astype(BF16))
    hid = (hg * _sigmoid(hg) * hu).astype(BF16)
    contrib = _dot(hid, wd_ref[...].astype(BF16))

    @pl.when(f == 0)
    def _first():
        acc_sc[...] = contrib

    @pl.when(f > 0)
    def _rest():
        acc_sc[...] += contrib

    @pl.when(f == pl.num_programs(2) - 1)
    def _store():
        o_ref[...] = acc_sc[...].astype(BF16)


def _experts(xs, w_gate, w_up, w_down, layer):
    n_exp, rows, d = xs.shape
    ff = w_gate.shape[3]
    tr = rows // 2 if (rows // 2) % (2 * SUBLANES) == 0 and rows % 2 == 0 else rows
    tf = 256
    return pl.pallas_call(
        _expert_kernel,
        grid=(n_exp, rows // tr, ff // tf),
        in_specs=[
            pl.BlockSpec((None, tr, d), lambda e, i, f: (e, i, 0)),
            pl.BlockSpec((None, None, d, tf), lambda e, i, f: (layer, e, 0, f)),
            pl.BlockSpec((None, None, d, tf), lambda e, i, f: (layer, e, 0, f)),
            pl.BlockSpec((None, None, tf, d), lambda e, i, f: (layer, e, f, 0)),
        ],
        out_specs=pl.BlockSpec((None, tr, d), lambda e, i, f: (e, i, 0)),
        out_shape=jax.ShapeDtypeStruct((n_exp, rows, d), BF16),
        scratch_shapes=[pltpu.VMEM((tr, d), F32)],
        compiler_params=_cparams(("arbitrary", "arbitrary", "arbitrary"), 58),
        name="moe_experts",
    )(xs, w_gate, w_up, w_down)


def _scatter_kernel(keyt_ref, afft_ref, ys_ref, h_ref, gl_ref, gc_ref, gf_ref, o_ref, acc_sc,
                    *, n_slots, ctx_len, tm, final_norm):
    t = pl.program_id(1)
    e = pl.program_id(2)
    n_exp = keyt_ref.shape[1]
    lane = lax.broadcasted_iota(I32, (tm, n_exp), 1)
    pick = lane == e
    kcol = jnp.sum(jnp.where(pick, keyt_ref[...].astype(F32), 0.0), axis=1, keepdims=True)
    gcol = jnp.sum(jnp.where(pick, afft_ref[...], 0.0), axis=1, keepdims=True)
    slot = lax.broadcasted_iota(I32, (1, n_slots), 1).astype(F32)
    weights = jnp.where(kcol == slot, gcol, 0.0).astype(BF16)
    contrib = _dot(weights, ys_ref[...])

    @pl.when(e == 0)
    def _first():
        acc_sc[...] = contrib

    @pl.when(e > 0)
    def _rest():
        acc_sc[...] += contrib

    @pl.when(e == n_exp - 1)
    def _store():
        row = t * tm + lax.broadcasted_iota(I32, (tm, 1), 0)
        gate = jnp.where(row < ctx_len, gc_ref[...], gl_ref[...])
        hn = h_ref[...] + gate * acc_sc[...]
        if final_norm:
            hn = _rms(hn, gf_ref[...])
        o_ref[...] = hn


def _scatter(key_t, aff_tt, ys, h, mods3, gfinal, n_slots, ctx_len, final_norm):
    bsz, L, d = h.shape
    n_exp = key_t.shape[2]
    tm = L // 8
    kern = functools.partial(_scatter_kernel, n_slots=n_slots, ctx_len=ctx_len, tm=tm,
                             final_norm=final_norm)
    return pl.pallas_call(
        kern,
        grid=(bsz, L // tm, n_exp),
        in_specs=[
            pl.BlockSpec((None, tm, n_exp), lambda b, t, e: (b, t, 0)),
            pl.BlockSpec((None, tm, n_exp), lambda b, t, e: (b, t, 0)),
            pl.BlockSpec((None, None, n_slots, d), lambda b, t, e: (e, b, 0, 0)),
            pl.BlockSpec((None, tm, d), lambda b, t, e: (b, t, 0)),
            pl.BlockSpec((None, 1, d), lambda b, t, e: (b, 0, 5)),
            pl.BlockSpec((None, 1, d), lambda b, t, e: (bsz, 0, 5)),
            pl.BlockSpec((1, d), lambda b, t, e: (0, 0)),
        ],
        out_specs=pl.BlockSpec((None, tm, d), lambda b, t, e: (b, t, 0)),
        out_shape=jax.ShapeDtypeStruct((bsz, L, d), F32),
        scratch_shapes=[pltpu.VMEM((tm, d), F32)],
        compiler_params=_cparams(("arbitrary", "arbitrary", "arbitrary"), 48),
        name="moe_scatter",
    )(key_t, aff_tt, ys, h, mods3, mods3, gfinal)


def _rope_tables(ctx_len, seq):
    pos = np.arange(seq)
    freqs = ROPE_THETA ** (-jnp.arange(0, ROT_AXIS, 2, dtype=F32) / ROT_AXIS)

    def cs(p):
        ang = jnp.asarray(p, F32)[:, None] * freqs[None, :]
        return jnp.cos(ang), jnp.sin(ang)

    cr, sr = cs(pos // GRID_W)
    cc, sc = cs(pos % GRID_W)
    cos_l = jnp.concatenate([cr, cr, cc, cc], axis=1)
    sin_l = jnp.concatenate([-sr, sr, -sc, sc], axis=1)
    cos_t = jnp.concatenate([jnp.ones((ctx_len, HEAD_DIM), F32), cos_l], axis=0)
    sin_t = jnp.concatenate([jnp.zeros((ctx_len, HEAD_DIM), F32), sin_l], axis=0)
    return cos_t, sin_t


def _moe(h, b_act, aff_t, mods3, w_gate, w_up, w_down, layer, gfinal, ctx_len, moe_ctx, final_norm):
    bsz, L, d = h.shape
    seq = L - ctx_len
    n_exp = aff_t.shape[1]
    cap_l = max(1, EC_FACTOR * seq // n_exp)
    cap_c = max(1, EC_FACTOR * ctx_len // n_exp) if moe_ctx else 0
    n_slots = cap_l + cap_c
    segments = ((ctx_len, seq, cap_l, 0), (0, ctx_len, cap_c, cap_l))
    key = _route(aff_t, segments)
    xs = _gather(key, b_act, n_slots)
    ys = _experts(xs.reshape(n_exp, bsz * n_slots, d), w_gate, w_up, w_down, layer)
    ys = ys.reshape(n_exp, bsz, n_slots, d)
    return _scatter(key.transpose(0, 2, 1), aff_t.transpose(0, 2, 1), ys, h, mods3, gfinal,
                    n_slots, ctx_len, final_norm)


def kernel(x, c, ctx, c_ctx, mod_w, mod_b, norm1_g, norm2_g, ev_w_in, ev_q_norm, ev_k_norm, ev_conv_w, ev_conv_b, ev_w_out, od_w_in, od_conv_w, od_conv_b, od_wa, od_ba, od_wx, od_bx, od_lam, od_w_out, moe_router, moe_w_gate, moe_w_up, moe_w_down, final_norm_g):
    bsz, seq, d = x.shape
    ctx_len = ctx.shape[1]
    depth = mod_w.shape[0]
    assert bsz < SUBLANES and ctx_len % ROW_TILE == 0 and seq % KV_CHUNK == 0
    attn_w = d // 2
    kv_w = attn_w // Q_PER_KV
    conv_w = d // 2
    lru_w = od_w_in.shape[2] // 2

    h = jnp.concatenate([ctx, x], axis=1)
    cvec = jnp.concatenate([c, c_ctx[None, :], jnp.zeros((SUBLANES - bsz - 1, d), F32)], axis=0)
    mods = _mods(cvec, mod_w, mod_b)
    cos_t, sin_t = _rope_tables(ctx_len, seq)
    gfinal = final_norm_g.reshape(1, d)

    for i in range(depth):
        last = i == depth - 1
        mods3 = mods[i].reshape(SUBLANES, 1, 6 * d)
        g1 = norm1_g[i].reshape(1, d)
        g2 = norm2_g[i].reshape(1, d)
        wr = moe_router[i].T
        wr_hi = wr.astype(BF16)
        wr_lo = (wr - wr_hi.astype(F32)).astype(BF16)
        j = i // 2
        if i % 2 == 0:
            q, k, v, gb, p = _even_in(h, mods3, g1, ev_w_in[j].astype(BF16),
                                      ev_q_norm[j].reshape(1, HEAD_DIM), ev_k_norm[j].reshape(1, HEAD_DIM),
                                      cos_t, sin_t, ctx_len, attn_w, kv_w, conv_w)
            att = _attention(q, k, v, ctx_len)
            h, b_act, aff_t = _even_out(h, att, gb, p, ev_conv_w[j], ev_conv_b[j].reshape(1, conv_w),
                                        ev_w_out[j].astype(BF16), mods3, g2, wr_hi, wr_lo, ctx_len)
        else:
            xpre, gg = _odd_in(h, mods3, g1, od_w_in[j].astype(BF16), ctx_len, lru_w)
            y2 = _lru(xpre, od_conv_w[j], od_conv_b[j].reshape(1, lru_w),
                      od_wa[j].astype(BF16), od_wx[j].astype(BF16),
                      od_ba[j].reshape(2, 1, lru_w), od_bx[j].reshape(2, 1, lru_w),
                      od_lam[j].reshape(2, 1, lru_w), ctx_len)
            h, b_act, aff_t = _odd_out(h, y2, gg, od_w_out[j].astype(BF16), mods3, g2, wr_hi, wr_lo, ctx_len)
        h = _moe(h, b_act, aff_t, mods3, moe_w_gate, moe_w_up, moe_w_down, i, gfinal,
                 ctx_len, moe_ctx=not last, final_norm=last)
    return h[:, ctx_len:, :]
```

```python
import functools

import jax
import jax.numpy as jnp
import numpy as np
from jax import lax
from jax.experimental import pallas as pl
from jax.experimental.pallas import tpu as pltpu

F32 = jnp.float32
BF16 = jnp.bfloat16
I32 = jnp.int32

EPS = 1e-6
HEAD_DIM = 128
ROT_AXIS = HEAD_DIM // 2
ROPE_THETA = 10000.0
GRID_W = 64
Q_PER_KV = 4
LRU_BLOCK = 256
LRU_C = 8.0
N_EXPERTS = 16
EC_FACTOR = 2

LANES = 128
SUBLANES = 8
ROW_TILE = 256
KV_CHUNK = 512
MIB = 1024 * 1024


def _cparams(sem, vmem_mib):
    return pltpu.CompilerParams(dimension_semantics=sem, vmem_limit_bytes=int(vmem_mib * MIB))


def _sigmoid(x):
    return 1.0 / (1.0 + jnp.exp(-x))


def _rms(x, g):
    return x * lax.rsqrt(jnp.mean(x * x, axis=-1, keepdims=True) + EPS) * g


def _dot(a, b):
    return jnp.dot(a, b, preferred_element_type=F32)


def _dot_nt(a, b):
    return lax.dot_general(a, b, (((1,), (1,)), ((), ())), preferred_element_type=F32)


def _resident(shape):
    nd = len(shape)
    return pl.BlockSpec(shape, lambda *_: (0,) * nd)


def _mods_kernel(c_ref, w_ref, b_ref, o_ref):
    c = c_ref[...]
    s = (c * _sigmoid(c)).astype(BF16)
    o_ref[...] = _dot(s, w_ref[...].astype(BF16)) + b_ref[...]


def _mods(cvec, mod_w, mod_b):
    depth, d, n6 = mod_w.shape
    tn = 1536
    return pl.pallas_call(
        _mods_kernel,
        grid=(depth, n6 // tn),
        in_specs=[
            pl.BlockSpec((SUBLANES, d), lambda l, j: (0, 0)),
            pl.BlockSpec((None, d, tn), lambda l, j: (l, 0, j)),
            pl.BlockSpec((None, 1, tn), lambda l, j: (l, 0, j)),
        ],
        out_specs=pl.BlockSpec((None, SUBLANES, tn), lambda l, j: (l, 0, j)),
        out_shape=jax.ShapeDtypeStruct((depth, SUBLANES, n6), F32),
        compiler_params=_cparams(("arbitrary", "arbitrary"), 40),
        name="adaln_mods",
    )(cvec, mod_w, mod_b.reshape(depth, 1, n6))


def _mod_spec(bsz, n_ctx_tiles, d, k):
    return pl.BlockSpec((None, 1, d), lambda b, i: (jnp.where(i < n_ctx_tiles, bsz, b), 0, k))


def _even_in_kernel(h_ref, g1_ref, sh_ref, sc_ref, w_ref, qg_ref, kg_ref, cos_ref, sin_ref,
                    q_ref, k_ref, v_ref, gb_ref, p_ref, *, attn_w, kv_w, conv_w):
    x = h_ref[...]
    a = (_rms(x, g1_ref[...]) * (1.0 + sc_ref[...]) + sh_ref[...]).astype(BF16)
    cos = cos_ref[...]
    sin = sin_ref[...]
    tm = x.shape[0]
    lane = lax.broadcasted_iota(I32, (tm, HEAD_DIM), 1)
    low_half = (lane % ROT_AXIS) < (ROT_AXIS // 2)

    def norm_rope(t, g, scale):
        t = t * lax.rsqrt(jnp.mean(t * t, axis=-1, keepdims=True) + EPS) * g
        swapped = jnp.where(low_half, pltpu.roll(t, HEAD_DIM - ROT_AXIS // 2, 1),
                            pltpu.roll(t, ROT_AXIS // 2, 1))
        return (t * cos + swapped * sin) * scale

    cw = 4 * HEAD_DIM
    qg = qg_ref[...]
    kg = kg_ref[...]
    q_scale = HEAD_DIM ** -0.5 * np.log2(np.e)
    for c0 in range(0, attn_w, cw):
        y = _dot(a, w_ref[:, c0:c0 + cw])
        for j in range(cw // HEAD_DIM):
            t = norm_rope(y[:, j * HEAD_DIM:(j + 1) * HEAD_DIM], qg, q_scale)
            q_ref[:, c0 + j * HEAD_DIM:c0 + (j + 1) * HEAD_DIM] = t.astype(BF16)
    y = _dot(a, w_ref[:, attn_w:attn_w + 2 * kv_w])
    for j in range(kv_w // HEAD_DIM):
        t = norm_rope(y[:, j * HEAD_DIM:(j + 1) * HEAD_DIM], kg, 1.0)
        k_ref[:, j * HEAD_DIM:(j + 1) * HEAD_DIM] = t.astype(BF16)
    v_ref[...] = y[:, kv_w:2 * kv_w].T.astype(BF16)
    off_b = attn_w + 2 * kv_w
    off_c = off_b + conv_w
    off_u = off_c + conv_w
    for c0 in range(0, conv_w, cw):
        gb_ref[:, c0:c0 + cw] = _dot(a, w_ref[:, off_b + c0:off_b + c0 + cw]).astype(BF16)
        cc = _dot(a, w_ref[:, off_c + c0:off_c + c0 + cw])
        uu = _dot(a, w_ref[:, off_u + c0:off_u + c0 + cw])
        p_ref[:, c0:c0 + cw] = (cc * uu).astype(BF16)


def _even_in(h, mods3, g1, w_in, qg, kg, cos_t, sin_t, ctx_len, attn_w, kv_w, conv_w):
    bsz, L, d = h.shape
    tm = ROW_TILE
    nct = ctx_len // tm
    tok = lambda width: pl.BlockSpec((None, tm, width), lambda b, i: (b, i, 0))
    kern = functools.partial(_even_in_kernel, attn_w=attn_w, kv_w=kv_w, conv_w=conv_w)
    return pl.pallas_call(
        kern,
        grid=(bsz, L // tm),
        in_specs=[
            tok(d),
            _resident((1, d)),
            _mod_spec(bsz, nct, d, 0),
            _mod_spec(bsz, nct, d, 1),
            _resident(w_in.shape),
            _resident((1, HEAD_DIM)),
            _resident((1, HEAD_DIM)),
            pl.BlockSpec((tm, HEAD_DIM), lambda b, i: (i, 0)),
            pl.BlockSpec((tm, HEAD_DIM), lambda b, i: (i, 0)),
        ],
        out_specs=[tok(attn_w), tok(kv_w), pl.BlockSpec((None, kv_w, tm), lambda b, i: (b, 0, i)),
                   tok(conv_w), tok(conv_w)],
        out_shape=[
            jax.ShapeDtypeStruct((bsz, L, attn_w), BF16),
            jax.ShapeDtypeStruct((bsz, L, kv_w), BF16),
            jax.ShapeDtypeStruct((bsz, kv_w, L), BF16),
            jax.ShapeDtypeStruct((bsz, L, conv_w), BF16),
            jax.ShapeDtypeStruct((bsz, L, conv_w), BF16),
        ],
        compiler_params=_cparams(("arbitrary", "arbitrary"), 56),
        name="even_in_proj",
    )(h, g1, mods3, mods3, w_in, qg, kg, cos_t, sin_t)


def _attn_kernel(q_ref, k_ref, vt_ref, o_ref, m_sc, l_sc, acc_sc, *, ctx_len, n_lat_chunks, tq):
    qi = pl.program_id(2)
    q = q_ref[...]
    qs = jnp.concatenate([q[:, j * HEAD_DIM:(j + 1) * HEAD_DIM] for j in range(Q_PER_KV)], axis=0)

    s = _dot_nt(k_ref[0:ctx_len, :], qs)
    m0 = jnp.max(s, axis=0, keepdims=True)
    p = jnp.exp2(s - m0)
    m_sc[...] = m0
    l_sc[...] = jnp.sum(p, axis=0, keepdims=True)
    acc_sc[...] = _dot(vt_ref[:, 0:ctx_len], p.astype(BF16))

    @pl.when(qi >= ctx_len // tq)
    def _latent_keys():
        def body(c, carry):
            start = pl.multiple_of(ctx_len + c * KV_CHUNK, KV_CHUNK // 2)
            sc = _dot_nt(k_ref[pl.ds(start, KV_CHUNK), :], qs)
            m_old = m_sc[...]
            m_new = jnp.maximum(m_old, jnp.max(sc, axis=0, keepdims=True))
            alpha = jnp.exp2(m_old - m_new)
            pc = jnp.exp2(sc - m_new)
            l_sc[...] = alpha * l_sc[...] + jnp.sum(pc, axis=0, keepdims=True)
            acc_sc[...] = alpha * acc_sc[...] + _dot(vt_ref[:, pl.ds(start, KV_CHUNK)], pc.astype(BF16))
            m_sc[...] = m_new
            return carry
        lax.fori_loop(0, n_lat_chunks, body, 0)

    o = acc_sc[...] * (1.0 / l_sc[...])
    for j in range(Q_PER_KV):
        o_ref[:, j * HEAD_DIM:(j + 1) * HEAD_DIM] = o[:, j * tq:(j + 1) * tq].T.astype(BF16)


def _attention(q, k, vt, ctx_len):
    bsz, L, attn_w = q.shape
    n_kv = k.shape[2] // HEAD_DIM
    tq = ROW_TILE
    gw = Q_PER_KV * HEAD_DIM
    seq = L - ctx_len
    kern = functools.partial(_attn_kernel, ctx_len=ctx_len, n_lat_chunks=seq // KV_CHUNK, tq=tq)
    return pl.pallas_call(
        kern,
        grid=(bsz, n_kv, L // tq),
        in_specs=[
            pl.BlockSpec((None, tq, gw), lambda b, g, i: (b, i, g)),
            pl.BlockSpec((None, L, HEAD_DIM), lambda b, g, i: (b, 0, g)),
            pl.BlockSpec((None, HEAD_DIM, L), lambda b, g, i: (b, g, 0)),
        ],
        out_specs=pl.BlockSpec((None, tq, gw), lambda b, g, i: (b, i, g)),
        out_shape=jax.ShapeDtypeStruct((bsz, L, attn_w), BF16),
        scratch_shapes=[
            pltpu.VMEM((1, Q_PER_KV * tq), F32),
            pltpu.VMEM((1, Q_PER_KV * tq), F32),
            pltpu.VMEM((HEAD_DIM, Q_PER_KV * tq), F32),
        ],
        compiler_params=_cparams(("arbitrary", "arbitrary", "arbitrary"), 40),
        name="attention",
    )(q, k, vt)


def _gelu_tanh(x):
    return 0.5 * x * (1.0 + jnp.tanh(np.sqrt(2.0 / np.pi) * (x + 0.044715 * (x * x * x))))


def _odd_in_kernel(h_ref, g1_ref, sh_ref, sc_ref, w_ref, x_ref, gg_ref, *, lru_w):
    x = h_ref[...]
    a = (_rms(x, g1_ref[...]) * (1.0 + sc_ref[...]) + sh_ref[...]).astype(BF16)
    cw = 512
    for c0 in range(0, lru_w, cw):
        x_ref[:, c0:c0 + cw] = _dot(a, w_ref[:, c0:c0 + cw])
        gg_ref[:, c0:c0 + cw] = _gelu_tanh(_dot(a, w_ref[:, lru_w + c0:lru_w + c0 + cw])).astype(BF16)


def _odd_in(h, mods3, g1, w_in, ctx_len, lru_w):
    bsz, L, d = h.shape
    tm = ROW_TILE
    nct = ctx_len // tm
    tok = lambda width: pl.BlockSpec((None, tm, width), lambda b, i: (b, i, 0))
    return pl.pallas_call(
        functools.partial(_odd_in_kernel, lru_w=lru_w),
        grid=(bsz, L // tm),
        in_specs=[tok(d), _resident((1, d)), _mod_spec(bsz, nct, d, 0), _mod_spec(bsz, nct, d, 1),
                  _resident(w_in.shape)],
        out_specs=[tok(lru_w), tok(lru_w)],
        out_shape=[jax.ShapeDtypeStruct((bsz, L, lru_w), F32),
                   jax.ShapeDtypeStruct((bsz, L, lru_w), BF16)],
        compiler_params=_cparams(("arbitrary", "arbitrary"), 56),
        name="odd_in_proj",
    )(h, g1, mods3, mods3, w_in)


def _lru_tile(d, j, n_tiles, n_ctx_tiles):
    rev = jnp.where(j < n_ctx_tiles, n_ctx_tiles - 1 - j, n_tiles - 1 - (j - n_ctx_tiles))
    return jnp.where(d == 0, j, rev)


def _lru_kernel(x_ref, xp_ref, xn_ref, cw_ref, cb_ref, wa_ref, wx_ref, ba_ref, bx_ref, lam_ref,
                y_ref, a_sc, b_sc, y_sc, h_sc, *, n_tiles, n_ctx_tiles, tt):
    d = pl.program_id(1)
    j = pl.program_id(2)
    tile = _lru_tile(d, j, n_tiles, n_ctx_tiles)
    width = x_ref.shape[1]
    x = x_ref[...]
    seg_first = jnp.logical_or(tile == 0, tile == n_ctx_tiles)
    seg_last = jnp.logical_or(tile == n_ctx_tiles - 1, tile == n_tiles - 1)
    prev = jnp.where(seg_first, 0.0, xp_ref[SUBLANES - 1:SUBLANES, :])
    nxt1 = jnp.where(seg_last, 0.0, xn_ref[0:1, :])
    nxt2 = jnp.where(seg_last, 0.0, xn_ref[1:2, :])
    row = lax.broadcasted_iota(I32, (tt, 1), 0)
    x_m1 = jnp.where(row == 0, prev, pltpu.roll(x, 1, 0))
    x_p1 = jnp.where(row == tt - 1, nxt1, pltpu.roll(x, tt - 1, 0))
    x_p2 = jnp.where(row == tt - 1, nxt2, jnp.where(row == tt - 2, nxt1, pltpu.roll(x, tt - 2, 0)))
    u = (x_m1 * cw_ref[0:1, :] + x * cw_ref[1:2, :] + x_p1 * cw_ref[2:3, :] + x_p2 * cw_ref[3:4, :]
         + cb_ref[...])
    z = -lam_ref[...]
    c8 = LRU_C * (jnp.maximum(z, 0.0) + jnp.log(1.0 + jnp.exp(-jnp.abs(z))))
    for blk in range(width // LRU_BLOCK):
        cs = slice(blk * LRU_BLOCK, (blk + 1) * LRU_BLOCK)
        ub = u[:, cs]
        ub16 = ub.astype(BF16)
        r = _sigmoid(_dot(ub16, wa_ref[blk]) + ba_ref[:, cs])
        ig = _sigmoid(_dot(ub16, wx_ref[blk]) + bx_ref[:, cs])
        a = jnp.exp(-r * c8[:, cs])
        a_sc[:, cs] = a
        b_sc[:, cs] = jnp.sqrt(1.0 - a * a) * ig * ub

    @pl.when(j == 0)
    def _reset():
        h_sc[...] = jnp.zeros_like(h_sc)

    sub = lax.broadcasted_iota(I32, (SUBLANES, width), 0)
    n_groups = tt // SUBLANES

    def group_scan(base, forward):
        av = a_sc[pl.ds(base, SUBLANES), :]
        bv = b_sc[pl.ds(base, SUBLANES), :]
        for s in (1, 2, 4):
            shift = s if forward else SUBLANES - s
            keep = (sub >= s) if forward else (sub < SUBLANES - s)
            a_sh = pltpu.roll(av, shift, 0)
            b_sh = pltpu.roll(bv, shift, 0)
            bv = jnp.where(keep, av * b_sh + bv, bv)
            av = jnp.where(keep, av * a_sh, av)
        out = av * h_sc[...] + bv
        y_sc[pl.ds(base, SUBLANES), :] = out
        last = out[SUBLANES - 1:SUBLANES, :] if forward else out[0:1, :]
        h_sc[...] = jnp.broadcast_to(last, (SUBLANES, width))

    @pl.when(d == 0)
    def _forward():
        def body(g, carry):
            group_scan(pl.multiple_of(g * SUBLANES, SUBLANES), True)
            return carry
        lax.fori_loop(0, n_groups, body, 0)

    @pl.when(d == 1)
    def _backward():
        def body(g, carry):
            group_scan(pl.multiple_of((n_groups - 1 - g) * SUBLANES, SUBLANES), False)
            return carry
        lax.fori_loop(0, n_groups, body, 0)

    y_ref[...] = y_sc[...].astype(BF16)


def _lru(xpre, conv_w, conv_b, wa, wx, ba, bx, lam, ctx_len):
    bsz, L, width = xpre.shape
    tt = ROW_TILE
    nt = L // tt
    nct = ctx_len // tt
    nb = width // LRU_BLOCK
    hb = tt // SUBLANES
    tile = lambda d, j: _lru_tile(d, j, nt, nct)
    kern = functools.partial(_lru_kernel, n_tiles=nt, n_ctx_tiles=nct, tt=tt)
    dir_vec = pl.BlockSpec((None, 1, width), lambda b, d, j: (d, 0, 0))
    dir_mat = pl.BlockSpec((None, nb, LRU_BLOCK, LRU_BLOCK), lambda b, d, j: (d, 0, 0, 0))
    return pl.pallas_call(
        kern,
        grid=(bsz, 2, nt),
        in_specs=[
            pl.BlockSpec((None, tt, width), lambda b, d, j: (b, tile(d, j), 0)),
            pl.BlockSpec((None, SUBLANES, width),
                         lambda b, d, j: (b, jnp.maximum(tile(d, j) * hb - 1, 0), 0)),
            pl.BlockSpec((None, SUBLANES, width),
                         lambda b, d, j: (b, jnp.minimum((tile(d, j) + 1) * hb, L // SUBLANES - 1), 0)),
            pl.BlockSpec(conv_w.shape, lambda b, d, j: (0, 0)),
            pl.BlockSpec((1, width), lambda b, d, j: (0, 0)),
            dir_mat, dir_mat, dir_vec, dir_vec, dir_vec,
        ],
        out_specs=pl.BlockSpec((None, None, tt, width), lambda b, d, j: (d, b, tile(d, j), 0)),
        out_shape=jax.ShapeDtypeStruct((2, bsz, L, width), BF16),
        scratch_shapes=[pltpu.VMEM((tt, width), F32), pltpu.VMEM((tt, width), F32),
                        pltpu.VMEM((tt, width), F32), pltpu.VMEM((SUBLANES, width), F32)],
        compiler_params=_cparams(("arbitrary", "arbitrary", "arbitrary"), 48),
        name="rglru_scan",
    )(xpre, xpre, xpre, conv_w, conv_b, wa, wx, ba, bx, lam)


def _out_tail(h_ref, y, gate_ref, g2_ref, sh2_ref, sc2_ref, wrh_ref, wrl_ref, ho_ref, bo_ref, aff_ref):
    hn = h_ref[...] + gate_ref[...] * y
    ho_ref[...] = hn
    bn = _rms(hn, g2_ref[...]) * (1.0 + sc2_ref[...]) + sh2_ref[...]
    b_hi = bn.astype(BF16)
    bo_ref[...] = b_hi
    b_lo = (bn - b_hi.astype(F32)).astype(BF16)
    wrh = wrh_ref[...]
    logits = _dot_nt(wrh, b_hi) + _dot_nt(wrh, b_lo) + _dot_nt(wrl_ref[...], b_hi)
    ex = jnp.exp(logits - jnp.max(logits, axis=0, keepdims=True))
    aff_ref[...] = ex / jnp.sum(ex, axis=0, keepdims=True)


def _even_out_kernel(h_ref, att_ref, gb_ref, p_ref, pp_ref, pn_ref, cw_ref, cb_ref, w_ref,
                     gate_ref, g2_ref, sh2_ref, sc2_ref, wrh_ref, wrl_ref,
                     ho_ref, bo_ref, aff_ref, *, n_tiles, n_ctx_tiles, attn_w):
    i = pl.program_id(1)
    tm = p_ref.shape[0]
    p = p_ref[...].astype(F32)
    seg_first = jnp.logical_or(i == 0, i == n_ctx_tiles)
    seg_last = jnp.logical_or(i == n_ctx_tiles - 1, i == n_tiles - 1)
    prev = jnp.where(seg_first, 0.0, pp_ref[...].astype(F32)[2 * SUBLANES - 1:2 * SUBLANES, :])
    nxt = jnp.where(seg_last, 0.0, pn_ref[...].astype(F32)[0:1, :])
    row = lax.broadcasted_iota(I32, (tm, 1), 0)
    p_m1 = jnp.where(row == 0, prev, pltpu.roll(p, 1, 0))
    p_p1 = jnp.where(row == tm - 1, nxt, pltpu.roll(p, tm - 1, 0))
    conv = p_m1 * cw_ref[0:1, :] + p * cw_ref[1:2, :] + p_p1 * cw_ref[2:3, :] + cb_ref[...]
    act = (gb_ref[...].astype(F32) * conv).astype(BF16)
    y = _dot(att_ref[...], w_ref[0:attn_w, :]) + _dot(act, w_ref[attn_w:, :])
    _out_tail(h_ref, y, gate_ref, g2_ref, sh2_ref, sc2_ref, wrh_ref, wrl_ref, ho_ref, bo_ref, aff_ref)


def _odd_out_kernel(h_ref, yf_ref, yr_ref, gg_ref, w_ref,
                    gate_ref, g2_ref, sh2_ref, sc2_ref, wrh_ref, wrl_ref,
                    ho_ref, bo_ref, aff_ref):
    ysum = (yf_ref[...].astype(F32) + yr_ref[...].astype(F32)).astype(BF16)
    act = (ysum.astype(F32) * gg_ref[...].astype(F32)).astype(BF16)
    y = _dot(act, w_ref[...])
    _out_tail(h_ref, y, gate_ref, g2_ref, sh2_ref, sc2_ref, wrh_ref, wrl_ref, ho_ref, bo_ref, aff_ref)


def _out_common(bsz, L, d, ctx_len, n_exp):
    tm = ROW_TILE
    nct = ctx_len // tm
    tail_specs = [
        _mod_spec(bsz, nct, d, 2), _resident((1, d)), _mod_spec(bsz, nct, d, 3), _mod_spec(bsz, nct, d, 4),
        _resident((n_exp, d)), _resident((n_exp, d)),
    ]
    out_specs = [
        pl.BlockSpec((None, tm, d), lambda b, i: (b, i, 0)),
        pl.BlockSpec((None, tm, d), lambda b, i: (b, i, 0)),
        pl.BlockSpec((None, n_exp, tm), lambda b, i: (b, 0, i)),
    ]
    out_shape = [
        jax.ShapeDtypeStruct((bsz, L, d), F32),
        jax.ShapeDtypeStruct((bsz, L, d), BF16),
        jax.ShapeDtypeStruct((bsz, n_exp, L), F32),
    ]
    return tm, nct, tail_specs, out_specs, out_shape


def _even_out(h, att, gb, p, conv_w, conv_b, w_out, mods3, g2, wr_hi, wr_lo, ctx_len):
    bsz, L, d = h.shape
    n_exp = wr_hi.shape[0]
    tm, nct, tail_specs, out_specs, out_shape = _out_common(bsz, L, d, ctx_len, n_exp)
    attn_w = att.shape[2]
    conv_wd = p.shape[2]
    halo = 2 * SUBLANES
    hb = tm // halo
    tok = lambda width: pl.BlockSpec((None, tm, width), lambda b, i: (b, i, 0))
    kern = functools.partial(_even_out_kernel, n_tiles=L // tm, n_ctx_tiles=nct, attn_w=attn_w)
    return pl.pallas_call(
        kern,
        grid=(bsz, L // tm),
        in_specs=[
            tok(d), tok(attn_w), tok(conv_wd), tok(conv_wd),
            pl.BlockSpec((None, halo, conv_wd), lambda b, i: (b, jnp.maximum(i * hb - 1, 0), 0)),
            pl.BlockSpec((None, halo, conv_wd), lambda b, i: (b, jnp.minimum((i + 1) * hb, L // halo - 1), 0)),
            _resident(conv_w.shape), _resident((1, conv_wd)), _resident(w_out.shape),
        ] + tail_specs,
        out_specs=out_specs,
        out_shape=out_shape,
        compiler_params=_cparams(("arbitrary", "arbitrary"), 48),
        name="even_out_proj",
    )(h, att, gb, p, p, p, conv_w, conv_b, w_out, mods3, g2, mods3, mods3, wr_hi, wr_lo)


def _odd_out(h, y2, gg, w_out, mods3, g2, wr_hi, wr_lo, ctx_len):
    bsz, L, d = h.shape
    n_exp = wr_hi.shape[0]
    width = gg.shape[2]
    tm, nct, tail_specs, out_specs, out_shape = _out_common(bsz, L, d, ctx_len, n_exp)
    tok = lambda wd: pl.BlockSpec((None, tm, wd), lambda b, i: (b, i, 0))
    return pl.pallas_call(
        _odd_out_kernel,
        grid=(bsz, L // tm),
        in_specs=[
            tok(d),
            pl.BlockSpec((None, None, tm, width), lambda b, i: (0, b, i, 0)),
            pl.BlockSpec((None, None, tm, width), lambda b, i: (1, b, i, 0)),
            tok(width), _resident(w_out.shape),
        ] + tail_specs,
        out_specs=out_specs,
        out_shape=out_shape,
        compiler_params=_cparams(("arbitrary", "arbitrary"), 48),
        name="odd_out_proj",
    )(h, y2, y2, gg, w_out, mods3, g2, mods3, mods3, wr_hi, wr_lo)


def _route_kernel(aff_ref, key_ref, base_ref, *, segments, scatter_tile):
    n_exp, L = aff_ref.shape
    for (start, n, cap, slot_off) in segments:
        if cap == 0:
            key_ref[:, start:start + n] = jnp.full((n_exp, n), -1, I32)
            continue
        aff = aff_ref[:, start:start + n]
        bits = pltpu.bitcast(aff, I32)
        capf = jnp.float32(cap)

        def count(mask):
            return jnp.sum(jnp.where(mask, 1.0, 0.0), axis=1, keepdims=True)

        thr = jnp.zeros((n_exp, 1), I32)
        for bit in range(30, -1, -1):
            cand = thr | jnp.int32(1 << bit)
            thr = jnp.where(count(bits >= cand) >= capf, cand, thr)
        gt = bits > thr
        eq = bits == thr
        need = capf - count(gt)

        cwid = 256 if n % 256 == 0 else LANES
        ri = lax.broadcasted_iota(I32, (cwid, cwid), 0)
        ci = lax.broadcasted_iota(I32, (cwid, cwid), 1)
        upper = jnp.where(ri <= ci, 1.0, 0.0).astype(BF16)

        def cumsum_excl(xf):
            outs = []
            carry = jnp.zeros((n_exp, 1), F32)
            for c in range(n // cwid):
                xc = xf[:, c * cwid:(c + 1) * cwid]
                inc = _dot(xc.astype(BF16), upper)
                outs.append(inc - xc + carry)
                carry = carry + inc[:, cwid - 1:cwid]
            return jnp.concatenate(outs, axis=1)

        eqf = jnp.where(eq, 1.0, 0.0)
        sel = jnp.logical_or(gt, jnp.logical_and(eq, cumsum_excl(eqf) < need))
        self_f = jnp.where(sel, 1.0, 0.0)
        slot = cumsum_excl(self_f) + jnp.float32(slot_off)
        key_ref[:, start:start + n] = jnp.where(sel, slot, -1.0).astype(I32)

    taken = jnp.where(key_ref[...] >= 0, 1.0, 0.0)
    tok = lax.broadcasted_iota(I32, (n_exp, L), 1)
    lane = lax.broadcasted_iota(I32, (n_exp, LANES), 1)
    base = jnp.zeros((n_exp, LANES), F32)
    for k in range(1, L // scatter_tile + 1):
        cnt = jnp.sum(jnp.where(tok < k * scatter_tile, taken, 0.0), axis=1, keepdims=True)
        base = jnp.where(lane == k, cnt, base)
    base_ref[...] = base.astype(I32)


def _route(aff_t, segments, scatter_tile):
    bsz, n_exp, L = aff_t.shape
    return pl.pallas_call(
        functools.partial(_route_kernel, segments=segments, scatter_tile=scatter_tile),
        grid=(bsz,),
        in_specs=[pl.BlockSpec((None, n_exp, L), lambda b: (b, 0, 0))],
        out_specs=[pl.BlockSpec((None, n_exp, L), lambda b: (b, 0, 0)),
                   pl.BlockSpec((None, n_exp, LANES), lambda b: (b, 0, 0))],
        out_shape=[jax.ShapeDtypeStruct((bsz, n_exp, L), I32),
                   jax.ShapeDtypeStruct((bsz, n_exp, LANES), I32)],
        compiler_params=_cparams(("arbitrary",), 32),
        name="moe_route",
    )(aff_t)


def _gather_kernel(key_ref, x_ref, o_ref, *, n_slots):
    e = pl.program_id(1)
    krow = key_ref[pl.ds(e, 1), :]
    slot = lax.broadcasted_iota(I32, (n_slots, 1), 0)
    onehot = jnp.where(krow == slot, 1.0, 0.0).astype(BF16)
    o_ref[...] = _dot(onehot, x_ref[...]).astype(BF16)


def _gather(key, b_act, n_slots):
    bsz, n_exp, L = key.shape
    d = b_act.shape[2]
    return pl.pallas_call(
        functools.partial(_gather_kernel, n_slots=n_slots),
        grid=(bsz, n_exp),
        in_specs=[
            pl.BlockSpec((None, n_exp, L), lambda b, e: (b, 0, 0)),
            pl.BlockSpec((None, L, d), lambda b, e: (b, 0, 0)),
        ],
        out_specs=pl.BlockSpec((None, None, n_slots, d), lambda b, e: (e, b, 0, 0)),
        out_shape=jax.ShapeDtypeStruct((n_exp, bsz, n_slots, d), BF16),
        compiler_params=_cparams(("arbitrary", "arbitrary"), 56),
        name="moe_gather",
    )(key, b_act)


def _expert_kernel(x_ref, wg_ref, wu_ref, wd_ref, o_ref, acc_sc):
    f = pl.program_id(2)
    x = x_ref[...]
    hg = _dot(x, wg_ref[...].astype(BF16))
    hu = _dot(x, wu_ref[...].astype(BF16))
    hid = (hg * _sigmoid(hg) * hu).astype(BF16)
    contrib = _dot(hid, wd_ref[...].astype(BF16))

    @pl.when(f == 0)
    def _first():
        acc_sc[...] = contrib

    @pl.when(f > 0)
    def _rest():
        acc_sc[...] += contrib

    @pl.when(f == pl.num_programs(2) - 1)
    def _store():
        o_ref[...] = acc_sc[...].astype(BF16)


def _experts(xs, w_gate, w_up, w_down, layer):
    n_exp, rows, d = xs.shape
    ff = w_gate.shape[3]
    tr = rows // 2 if (rows // 2) % (2 * SUBLANES) == 0 and rows % 2 == 0 else rows
    tf = 256
    return pl.pallas_call(
        _expert_kernel,
        grid=(n_exp, rows // tr, ff // tf),
        in_specs=[
            pl.BlockSpec((None, tr, d), lambda e, i, f: (e, i, 0)),
            pl.BlockSpec((None, None, d, tf), lambda e, i, f: (layer, e, 0, f)),
            pl.BlockSpec((None, None, d, tf), lambda e, i, f: (layer, e, 0, f)),
            pl.BlockSpec((None, None, tf, d), lambda e, i, f: (layer, e, f, 0)),
        ],
        out_specs=pl.BlockSpec((None, tr, d), lambda e, i, f: (e, i, 0)),
        out_shape=jax.ShapeDtypeStruct((n_exp, rows, d), BF16),
        scratch_shapes=[pltpu.VMEM((tr, d), F32)],
        compiler_params=_cparams(("arbitrary", "arbitrary", "arbitrary"), 58),
        name="moe_experts",
    )(xs, w_gate, w_up, w_down)


SCATTER_WINDOW = 256
SCATTER_EXPERTS = 2
SCATTER_TILES = 4


def _scatter_kernel(base_ref, keyt_ref, afft_ref, ys_ref, h_ref, gl_ref, gc_ref, gf_ref, o_ref,
                    *, n_slots, ctx_len, tm, n_tiles, final_norm):
    b = pl.program_id(0)
    t = pl.program_id(1)
    eg = pl.program_id(2)
    n_exp = keyt_ref.shape[1]
    win = min(SCATTER_WINDOW, n_slots)
    lane = lax.broadcasted_iota(I32, (tm, n_exp), 1)
    keyt = keyt_ref[...].astype(F32)
    afft = afft_ref[...]
    total = None
    cols = []
    for k in range(SCATTER_EXPERTS):
        e = eg * SCATTER_EXPERTS + k
        pick = lane == e
        kcol = jnp.sum(jnp.where(pick, keyt, 0.0), axis=1, keepdims=True)
        gcol = jnp.sum(jnp.where(pick, afft, 0.0), axis=1, keepdims=True)
        lo = base_ref[(b * (n_tiles + 1) + t) * n_exp + e]
        hi = base_ref[(b * (n_tiles + 1) + t + 1) * n_exp + e]
        start = jnp.minimum((lo // (2 * SUBLANES)) * (2 * SUBLANES), n_slots - win)
        start = pl.multiple_of(start, 2 * SUBLANES)
        slot = (start + lax.broadcasted_iota(I32, (1, win), 1)).astype(F32)
        weights = jnp.where(kcol == slot, gcol, 0.0).astype(BF16)
        part = _dot(weights, ys_ref[k, pl.ds(start, win), :])
        total = part if total is None else total + part
        cols.append((kcol, gcol, start, hi))

    @pl.when(eg == 0)
    def _first():
        o_ref[...] = total

    @pl.when(eg > 0)
    def _rest():
        o_ref[...] += total

    for k, (kcol, gcol, start, hi) in enumerate(cols):
        @pl.when(hi > start + win)
        def _beyond_window():
            slot = lax.broadcasted_iota(I32, (1, n_slots), 1)
            outside = jnp.logical_or(slot < start, slot >= start + win)
            hit = jnp.logical_and(kcol == slot.astype(F32), outside)
            weights = jnp.where(hit, gcol, 0.0).astype(BF16)
            o_ref[...] += _dot(weights, ys_ref[k])

    @pl.when(eg == pl.num_programs(2) - 1)
    def _store():
        row = t * tm + lax.broadcasted_iota(I32, (tm, 1), 0)
        gate = jnp.where(row < ctx_len, gc_ref[...], gl_ref[...])
        hn = h_ref[...] + gate * o_ref[...]
        if final_norm:
            hn = _rms(hn, gf_ref[...])
        o_ref[...] = hn


def _scatter(base, key_t, aff_tt, ys, h, mods3, gfinal, n_slots, ctx_len, final_norm):
    bsz, L, d = h.shape
    n_exp = key_t.shape[2]
    n_tiles = SCATTER_TILES
    tm = L // n_tiles
    assert n_exp % SCATTER_EXPERTS == 0 and (n_slots - min(SCATTER_WINDOW, n_slots)) % (2 * SUBLANES) == 0
    kern = functools.partial(_scatter_kernel, n_slots=n_slots, ctx_len=ctx_len, tm=tm, n_tiles=n_tiles,
                             final_norm=final_norm)
    grid_spec = pltpu.PrefetchScalarGridSpec(
        num_scalar_prefetch=1,
        grid=(bsz, n_tiles, n_exp // SCATTER_EXPERTS),
        in_specs=[
            pl.BlockSpec((None, tm, n_exp), lambda b, t, e, *_: (b, t, 0)),
            pl.BlockSpec((None, tm, n_exp), lambda b, t, e, *_: (b, t, 0)),
            pl.BlockSpec((SCATTER_EXPERTS, None, n_slots, d), lambda b, t, e, *_: (e, b, 0, 0)),
            pl.BlockSpec((None, tm, d), lambda b, t, e, *_: (b, t, 0), pipeline_mode=pl.Buffered(1)),
            pl.BlockSpec((None, 1, d), lambda b, t, e, *_: (b, 0, 5)),
            pl.BlockSpec((None, 1, d), lambda b, t, e, *_: (bsz, 0, 5)),
            pl.BlockSpec((1, d), lambda b, t, e, *_: (0, 0)),
        ],
        out_specs=pl.BlockSpec((None, tm, d), lambda b, t, e, *_: (b, t, 0)),
    )
    return pl.pallas_call(
        kern,
        grid_spec=grid_spec,
        out_shape=jax.ShapeDtypeStruct((bsz, L, d), F32),
        compiler_params=_cparams(("arbitrary", "arbitrary", "arbitrary"), 56),
        name="moe_scatter",
    )(base, key_t, aff_tt, ys, h, mods3, mods3, gfinal)


def _rope_tables(ctx_len, seq):
    pos = np.arange(seq)
    freqs = ROPE_THETA ** (-jnp.arange(0, ROT_AXIS, 2, dtype=F32) / ROT_AXIS)

    def cs(p):
        ang = jnp.asarray(p, F32)[:, None] * freqs[None, :]
        return jnp.cos(ang), jnp.sin(ang)

    cr, sr = cs(pos // GRID_W)
    cc, sc = cs(pos % GRID_W)
    cos_l = jnp.concatenate([cr, cr, cc, cc], axis=1)
    sin_l = jnp.concatenate([-sr, sr, -sc, sc], axis=1)
    cos_t = jnp.concatenate([jnp.ones((ctx_len, HEAD_DIM), F32), cos_l], axis=0)
    sin_t = jnp.concatenate([jnp.zeros((ctx_len, HEAD_DIM), F32), sin_l], axis=0)
    return cos_t, sin_t


def _moe(h, b_act, aff_t, mods3, w_gate, w_up, w_down, layer, gfinal, ctx_len, moe_ctx, final_norm):
    bsz, L, d = h.shape
    seq = L - ctx_len
    n_exp = aff_t.shape[1]
    cap_l = max(1, EC_FACTOR * seq // n_exp)
    cap_c = max(1, EC_FACTOR * ctx_len // n_exp) if moe_ctx else 0
    n_slots = cap_l + cap_c
    segments = ((0, ctx_len, cap_c, 0), (ctx_len, seq, cap_l, cap_c))
    key, base = _route(aff_t, segments, L // SCATTER_TILES)
    xs = _gather(key, b_act, n_slots)
    ys = _experts(xs.reshape(n_exp, bsz * n_slots, d), w_gate, w_up, w_down, layer)
    ys = ys.reshape(n_exp, bsz, n_slots, d)
    base_flat = base[:, :, :SCATTER_TILES + 1].transpose(0, 2, 1).reshape(-1)
    return _scatter(base_flat, key.transpose(0, 2, 1), aff_t.transpose(0, 2, 1), ys, h, mods3, gfinal,
                    n_slots, ctx_len, final_norm)


def kernel(x, c, ctx, c_ctx, mod_w, mod_b, norm1_g, norm2_g, ev_w_in, ev_q_norm, ev_k_norm, ev_conv_w, ev_conv_b, ev_w_out, od_w_in, od_conv_w, od_conv_b, od_wa, od_ba, od_wx, od_bx, od_lam, od_w_out, moe_router, moe_w_gate, moe_w_up, moe_w_down, final_norm_g):
    bsz, seq, d = x.shape
    ctx_len = ctx.shape[1]
    depth = mod_w.shape[0]
    assert bsz < SUBLANES and ctx_len % ROW_TILE == 0 and seq % KV_CHUNK == 0
    attn_w = d // 2
    kv_w = attn_w // Q_PER_KV
    conv_w = d // 2
    lru_w = od_w_in.shape[2] // 2

    h = jnp.concatenate([ctx, x], axis=1)
    cvec = jnp.concatenate([c, c_ctx[None, :], jnp.zeros((SUBLANES - bsz - 1, d), F32)], axis=0)
    mods = _mods(cvec, mod_w, mod_b)
    cos_t, sin_t = _rope_tables(ctx_len, seq)
    gfinal = final_norm_g.reshape(1, d)

    for i in range(depth):
        last = i == depth - 1
        mods3 = mods[i].reshape(SUBLANES, 1, 6 * d)
        g1 = norm1_g[i].reshape(1, d)
        g2 = norm2_g[i].reshape(1, d)
        wr = moe_router[i].T
        wr_hi = wr.astype(BF16)
        wr_lo = (wr - wr_hi.astype(F32)).astype(BF16)
        j = i // 2
        if i % 2 == 0:
            q, k, vt, gb, p = _even_in(h, mods3, g1, ev_w_in[j].astype(BF16),
                                      ev_q_norm[j].reshape(1, HEAD_DIM), ev_k_norm[j].reshape(1, HEAD_DIM),
                                      cos_t, sin_t, ctx_len, attn_w, kv_w, conv_w)
            att = _attention(q, k, vt, ctx_len)
            h, b_act, aff_t = _even_out(h, att, gb, p, ev_conv_w[j], ev_conv_b[j].reshape(1, conv_w),
                                        ev_w_out[j].astype(BF16), mods3, g2, wr_hi, wr_lo, ctx_len)
        else:
            xpre, gg = _odd_in(h, mods3, g1, od_w_in[j].astype(BF16), ctx_len, lru_w)
            y2 = _lru(xpre, od_conv_w[j], od_conv_b[j].reshape(1, lru_w),
                      od_wa[j].astype(BF16), od_wx[j].astype(BF16),
                      od_ba[j].reshape(2, 1, lru_w), od_bx[j].reshape(2, 1, lru_w),
                      od_lam[j].reshape(2, 1, lru_w), ctx_len)
            h, b_act, aff_t = _odd_out(h, y2, gg, od_w_out[j].astype(BF16), mods3, g2, wr_hi, wr_lo, ctx_len)
        h = _moe(h, b_act, aff_t, mods3, moe_w_gate, moe_w_up, moe_w_down, i, gfinal,
                 ctx_len, moe_ctx=not last, final_norm=last)
    return h[:, ctx_len:, :]
```

```python
import functools

import jax
import jax.numpy as jnp
import numpy as np
from jax import lax
from jax.experimental import pallas as pl
from jax.experimental.pallas import tpu as pltpu

F32 = jnp.float32
BF16 = jnp.bfloat16
I32 = jnp.int32

EPS = 1e-6
HEAD_DIM = 128
ROT_AXIS = HEAD_DIM // 2
ROPE_THETA = 10000.0
GRID_W = 64
Q_PER_KV = 4
LRU_BLOCK = 256
LRU_C = 8.0
N_EXPERTS = 16
EC_FACTOR = 2

LANES = 128
SUBLANES = 8
ROW_TILE = 256
KV_CHUNK = 512
MIB = 1024 * 1024


def _cparams(sem, vmem_mib):
    return pltpu.CompilerParams(dimension_semantics=sem, vmem_limit_bytes=int(vmem_mib * MIB))


def _sigmoid(x):
    return 1.0 / (1.0 + jnp.exp(-x))


def _rms(x, g):
    return x * lax.rsqrt(jnp.mean(x * x, axis=-1, keepdims=True) + EPS) * g


def _dot(a, b):
    return jnp.dot(a, b, preferred_element_type=F32)


def _dot_nt(a, b):
    return lax.dot_general(a, b, (((1,), (1,)), ((), ())), preferred_element_type=F32)


def _resident(shape):
    nd = len(shape)
    return pl.BlockSpec(shape, lambda *_: (0,) * nd)


def _mods_kernel(c_ref, w_ref, b_ref, o_ref):
    c = c_ref[...]
    s = (c * _sigmoid(c)).astype(BF16)
    o_ref[...] = _dot(s, w_ref[...].astype(BF16)) + b_ref[...]


def _mods(cvec, mod_w, mod_b):
    depth, d, n6 = mod_w.shape
    tn = 1536
    return pl.pallas_call(
        _mods_kernel,
        grid=(depth, n6 // tn),
        in_specs=[
            pl.BlockSpec((SUBLANES, d), lambda l, j: (0, 0)),
            pl.BlockSpec((None, d, tn), lambda l, j: (l, 0, j)),
            pl.BlockSpec((None, 1, tn), lambda l, j: (l, 0, j)),
        ],
        out_specs=pl.BlockSpec((None, SUBLANES, tn), lambda l, j: (l, 0, j)),
        out_shape=jax.ShapeDtypeStruct((depth, SUBLANES, n6), F32),
        compiler_params=_cparams(("arbitrary", "arbitrary"), 40),
        name="adaln_mods",
    )(cvec, mod_w, mod_b.reshape(depth, 1, n6))


def _mod_spec(bsz, n_ctx_tiles, d, k):
    return pl.BlockSpec((None, 1, d), lambda b, i: (jnp.where(i < n_ctx_tiles, bsz, b), 0, k))


def _even_in_kernel(h_ref, g1_ref, sh_ref, sc_ref, w_ref, qg_ref, kg_ref, cos_ref, sin_ref,
                    q_ref, k_ref, v_ref, gb_ref, p_ref, *, attn_w, kv_w, conv_w):
    x = h_ref[...]
    a = (_rms(x, g1_ref[...]) * (1.0 + sc_ref[...]) + sh_ref[...]).astype(BF16)
    cos = cos_ref[...]
    sin = sin_ref[...]
    tm = x.shape[0]
    lane = lax.broadcasted_iota(I32, (tm, HEAD_DIM), 1)
    low_half = (lane % ROT_AXIS) < (ROT_AXIS // 2)

    def norm_rope(t, g, scale):
        t = t * lax.rsqrt(jnp.mean(t * t, axis=-1, keepdims=True) + EPS) * g
        swapped = jnp.where(low_half, pltpu.roll(t, HEAD_DIM - ROT_AXIS // 2, 1),
                            pltpu.roll(t, ROT_AXIS // 2, 1))
        return (t * cos + swapped * sin) * scale

    cw = 4 * HEAD_DIM
    qg = qg_ref[...]
    kg = kg_ref[...]
    q_scale = HEAD_DIM ** -0.5 * np.log2(np.e)
    for c0 in range(0, attn_w, cw):
        y = _dot(a, w_ref[:, c0:c0 + cw])
        for j in range(cw // HEAD_DIM):
            t = norm_rope(y[:, j * HEAD_DIM:(j + 1) * HEAD_DIM], qg, q_scale)
            q_ref[:, c0 + j * HEAD_DIM:c0 + (j + 1) * HEAD_DIM] = t.astype(BF16)
    y = _dot(a, w_ref[:, attn_w:attn_w + 2 * kv_w])
    for j in range(kv_w // HEAD_DIM):
        t = norm_rope(y[:, j * HEAD_DIM:(j + 1) * HEAD_DIM], kg, 1.0)
        k_ref[:, j * HEAD_DIM:(j + 1) * HEAD_DIM] = t.astype(BF16)
    v_ref[...] = y[:, kv_w:2 * kv_w].T.astype(BF16)
    off_b = attn_w + 2 * kv_w
    off_c = off_b + conv_w
    off_u = off_c + conv_w
    for c0 in range(0, conv_w, cw):
        gb_ref[:, c0:c0 + cw] = _dot(a, w_ref[:, off_b + c0:off_b + c0 + cw]).astype(BF16)
        cc = _dot(a, w_ref[:, off_c + c0:off_c + c0 + cw])
        uu = _dot(a, w_ref[:, off_u + c0:off_u + c0 + cw])
        p_ref[:, c0:c0 + cw] = (cc * uu).astype(BF16)


def _even_in(h, mods3, g1, w_in, qg, kg, cos_t, sin_t, ctx_len, attn_w, kv_w, conv_w):
    bsz, L, d = h.shape
    tm = ROW_TILE
    nct = ctx_len // tm
    tok = lambda width: pl.BlockSpec((None, tm, width), lambda b, i: (b, i, 0))
    kern = functools.partial(_even_in_kernel, attn_w=attn_w, kv_w=kv_w, conv_w=conv_w)
    return pl.pallas_call(
        kern,
        grid=(bsz, L // tm),
        in_specs=[
            tok(d),
            _resident((1, d)),
            _mod_spec(bsz, nct, d, 0),
            _mod_spec(bsz, nct, d, 1),
            _resident(w_in.shape),
            _resident((1, HEAD_DIM)),
            _resident((1, HEAD_DIM)),
            pl.BlockSpec((tm, HEAD_DIM), lambda b, i: (i, 0)),
            pl.BlockSpec((tm, HEAD_DIM), lambda b, i: (i, 0)),
        ],
        out_specs=[tok(attn_w), tok(kv_w), pl.BlockSpec((None, kv_w, tm), lambda b, i: (b, 0, i)),
                   tok(conv_w), tok(conv_w)],
        out_shape=[
            jax.ShapeDtypeStruct((bsz, L, attn_w), BF16),
            jax.ShapeDtypeStruct((bsz, L, kv_w), BF16),
            jax.ShapeDtypeStruct((bsz, kv_w, L), BF16),
            jax.ShapeDtypeStruct((bsz, L, conv_w), BF16),
            jax.ShapeDtypeStruct((bsz, L, conv_w), BF16),
        ],
        compiler_params=_cparams(("arbitrary", "arbitrary"), 56),
        name="even_in_proj",
    )(h, g1, mods3, mods3, w_in, qg, kg, cos_t, sin_t)


def _attn_kernel(q_ref, k_ref, vt_ref, o_ref, m_sc, l_sc, acc_sc, sa_sc, sb_sc, *, ctx_len, n_lat_chunks, tq):
    qi = pl.program_id(2)
    q = q_ref[...]
    qs = jnp.concatenate([q[:, j * HEAD_DIM:(j + 1) * HEAD_DIM] for j in range(Q_PER_KV)], axis=0)

    def lat_start(c):
        return pl.multiple_of(ctx_len + c * KV_CHUNK, KV_CHUNK // 2)

    def scores(c):
        return _dot_nt(k_ref[pl.ds(lat_start(c), KV_CHUNK), :], qs)

    def absorb(s_ref, c):
        sc = s_ref[...]
        m_old = m_sc[...]
        m_new = jnp.maximum(m_old, jnp.max(sc, axis=0, keepdims=True))
        alpha = jnp.exp2(m_old - m_new)
        pc = jnp.exp2(sc - m_new)
        l_sc[...] = alpha * l_sc[...] + jnp.sum(pc, axis=0, keepdims=True)
        acc_sc[...] = alpha * acc_sc[...] + _dot(vt_ref[:, pl.ds(lat_start(c), KV_CHUNK)], pc.astype(BF16))
        m_sc[...] = m_new

    s = _dot_nt(k_ref[0:ctx_len, :], qs)
    m0 = jnp.max(s, axis=0, keepdims=True)
    p = jnp.exp2(s - m0)
    m_sc[...] = m0
    l_sc[...] = jnp.sum(p, axis=0, keepdims=True)
    acc_sc[...] = _dot(vt_ref[:, 0:ctx_len], p.astype(BF16))

    @pl.when(qi >= ctx_len // tq)
    def _latent_keys():
        sa_sc[...] = scores(0)

        def body(i, carry):
            c = 2 * i
            sb_sc[...] = scores(c + 1)
            absorb(sa_sc, c)
            sa_sc[...] = scores(jnp.minimum(c + 2, n_lat_chunks - 1))
            absorb(sb_sc, c + 1)
            return carry
        lax.fori_loop(0, n_lat_chunks // 2, body, 0)

    o = acc_sc[...] * (1.0 / l_sc[...])
    for j in range(Q_PER_KV):
        o_ref[:, j * HEAD_DIM:(j + 1) * HEAD_DIM] = o[:, j * tq:(j + 1) * tq].T.astype(BF16)


def _attention(q, k, vt, ctx_len):
    bsz, L, attn_w = q.shape
    n_kv = k.shape[2] // HEAD_DIM
    tq = ROW_TILE
    gw = Q_PER_KV * HEAD_DIM
    seq = L - ctx_len
    kern = functools.partial(_attn_kernel, ctx_len=ctx_len, n_lat_chunks=seq // KV_CHUNK, tq=tq)
    return pl.pallas_call(
        kern,
        grid=(bsz, n_kv, L // tq),
        in_specs=[
            pl.BlockSpec((None, tq, gw), lambda b, g, i: (b, i, g)),
            pl.BlockSpec((None, L, HEAD_DIM), lambda b, g, i: (b, 0, g)),
            pl.BlockSpec((None, HEAD_DIM, L), lambda b, g, i: (b, g, 0)),
        ],
        out_specs=pl.BlockSpec((None, tq, gw), lambda b, g, i: (b, i, g)),
        out_shape=jax.ShapeDtypeStruct((bsz, L, attn_w), BF16),
        scratch_shapes=[
            pltpu.VMEM((1, Q_PER_KV * tq), F32),
            pltpu.VMEM((1, Q_PER_KV * tq), F32),
            pltpu.VMEM((HEAD_DIM, Q_PER_KV * tq), F32),
            pltpu.VMEM((KV_CHUNK, Q_PER_KV * tq), F32),
            pltpu.VMEM((KV_CHUNK, Q_PER_KV * tq), F32),
        ],
        compiler_params=_cparams(("arbitrary", "arbitrary", "arbitrary"), 40),
        name="attention",
    )(q, k, vt)


def _gelu_tanh(x):
    return 0.5 * x * (1.0 + jnp.tanh(np.sqrt(2.0 / np.pi) * (x + 0.044715 * (x * x * x))))


def _odd_in_kernel(h_ref, g1_ref, sh_ref, sc_ref, w_ref, x_ref, gg_ref, *, lru_w):
    x = h_ref[...]
    a = (_rms(x, g1_ref[...]) * (1.0 + sc_ref[...]) + sh_ref[...]).astype(BF16)
    cw = 512
    for c0 in range(0, lru_w, cw):
        x_ref[:, c0:c0 + cw] = _dot(a, w_ref[:, c0:c0 + cw])
        gg_ref[:, c0:c0 + cw] = _gelu_tanh(_dot(a, w_ref[:, lru_w + c0:lru_w + c0 + cw])).astype(BF16)


def _odd_in(h, mods3, g1, w_in, ctx_len, lru_w):
    bsz, L, d = h.shape
    tm = ROW_TILE
    nct = ctx_len // tm
    tok = lambda width: pl.BlockSpec((None, tm, width), lambda b, i: (b, i, 0))
    return pl.pallas_call(
        functools.partial(_odd_in_kernel, lru_w=lru_w),
        grid=(bsz, L // tm),
        in_specs=[tok(d), _resident((1, d)), _mod_spec(bsz, nct, d, 0), _mod_spec(bsz, nct, d, 1),
                  _resident(w_in.shape)],
        out_specs=[tok(lru_w), tok(lru_w)],
        out_shape=[jax.ShapeDtypeStruct((bsz, L, lru_w), F32),
                   jax.ShapeDtypeStruct((bsz, L, lru_w), BF16)],
        compiler_params=_cparams(("arbitrary", "arbitrary"), 56),
        name="odd_in_proj",
    )(h, g1, mods3, mods3, w_in)


def _lru_tile(d, j, n_tiles, n_ctx_tiles):
    rev = jnp.where(j < n_ctx_tiles, n_ctx_tiles - 1 - j, n_tiles - 1 - (j - n_ctx_tiles))
    return jnp.where(d == 0, j, rev)


def _lru_kernel(x_ref, xp_ref, xn_ref, cw_ref, cb_ref, wa_ref, wx_ref, ba_ref, bx_ref, lam_ref,
                y_ref, a_sc, b_sc, y_sc, h_sc, *, n_tiles, n_ctx_tiles, tt):
    d = pl.program_id(1)
    j = pl.program_id(2)
    tile = _lru_tile(d, j, n_tiles, n_ctx_tiles)
    width = x_ref.shape[1]
    x = x_ref[...]
    seg_first = jnp.logical_or(tile == 0, tile == n_ctx_tiles)
    seg_last = jnp.logical_or(tile == n_ctx_tiles - 1, tile == n_tiles - 1)
    prev = jnp.where(seg_first, 0.0, xp_ref[SUBLANES - 1:SUBLANES, :])
    nxt1 = jnp.where(seg_last, 0.0, xn_ref[0:1, :])
    nxt2 = jnp.where(seg_last, 0.0, xn_ref[1:2, :])
    w0, w1, w2, w3 = (cw_ref[k:k + 1, :] for k in range(4))
    u = (pltpu.roll(x, 1, 0) * w0 + x * w1 + pltpu.roll(x, tt - 1, 0) * w2 + pltpu.roll(x, tt - 2, 0) * w3
         + cb_ref[...])
    sub8 = lax.broadcasted_iota(I32, (SUBLANES, 1), 0)
    fix_first = jnp.where(sub8 == 0, (prev - x[tt - 1:tt, :]) * w0, 0.0)
    fix_last = jnp.where(sub8 == SUBLANES - 1, (nxt1 - x[0:1, :]) * w2 + (nxt2 - x[1:2, :]) * w3,
                         jnp.where(sub8 == SUBLANES - 2, (nxt1 - x[0:1, :]) * w3, 0.0))
    u = jnp.concatenate([u[0:SUBLANES] + fix_first, u[SUBLANES:tt - SUBLANES], u[tt - SUBLANES:] + fix_last],
                        axis=0)
    z = -lam_ref[...]
    la = (-LRU_C * np.log2(np.e)) * (jnp.maximum(z, 0.0) + jnp.log(1.0 + jnp.exp(-jnp.abs(z))))
    for blk in range(width // LRU_BLOCK):
        cs = slice(blk * LRU_BLOCK, (blk + 1) * LRU_BLOCK)
        ub = u[:, cs]
        ub16 = ub.astype(BF16)
        r = _sigmoid(_dot(ub16, wa_ref[blk]) + ba_ref[:, cs])
        ig = _sigmoid(_dot(ub16, wx_ref[blk]) + bx_ref[:, cs])
        a = jnp.exp2(r * la[:, cs])
        a_sc[:, cs] = a
        b_sc[:, cs] = jnp.sqrt(1.0 - a * a) * ig * ub

    @pl.when(j == 0)
    def _reset():
        h_sc[...] = jnp.zeros_like(h_sc)

    sub = lax.broadcasted_iota(I32, (SUBLANES, width), 0)
    n_groups = tt // SUBLANES

    def group_scan(base, forward):
        av = a_sc[pl.ds(base, SUBLANES), :]
        bv = b_sc[pl.ds(base, SUBLANES), :]
        for s in (1, 2, 4):
            shift = s if forward else SUBLANES - s
            keep = (sub >= s) if forward else (sub < SUBLANES - s)
            a_sh = pltpu.roll(av, shift, 0)
            b_sh = pltpu.roll(bv, shift, 0)
            bv = jnp.where(keep, av * b_sh + bv, bv)
            av = jnp.where(keep, av * a_sh, av)
        out = av * h_sc[...] + bv
        y_sc[pl.ds(base, SUBLANES), :] = out
        last = out[SUBLANES - 1:SUBLANES, :] if forward else out[0:1, :]
        h_sc[...] = jnp.broadcast_to(last, (SUBLANES, width))

    @pl.when(d == 0)
    def _forward():
        def body(g, carry):
            group_scan(pl.multiple_of(g * SUBLANES, SUBLANES), True)
            return carry
        lax.fori_loop(0, n_groups, body, 0)

    @pl.when(d == 1)
    def _backward():
        def body(g, carry):
            group_scan(pl.multiple_of((n_groups - 1 - g) * SUBLANES, SUBLANES), False)
            return carry
        lax.fori_loop(0, n_groups, body, 0)

    y_ref[...] = y_sc[...].astype(BF16)


def _lru(xpre, conv_w, conv_b, wa, wx, ba, bx, lam, ctx_len):
    bsz, L, width = xpre.shape
    tt = ROW_TILE
    nt = L // tt
    nct = ctx_len // tt
    nb = width // LRU_BLOCK
    hb = tt // SUBLANES
    tile = lambda d, j: _lru_tile(d, j, nt, nct)
    kern = functools.partial(_lru_kernel, n_tiles=nt, n_ctx_tiles=nct, tt=tt)
    dir_vec = pl.BlockSpec((None, 1, width), lambda b, d, j: (d, 0, 0))
    dir_mat = pl.BlockSpec((None, nb, LRU_BLOCK, LRU_BLOCK), lambda b, d, j: (d, 0, 0, 0))
    return pl.pallas_call(
        kern,
        grid=(bsz, 2, nt),
        in_specs=[
            pl.BlockSpec((None, tt, width), lambda b, d, j: (b, tile(d, j), 0)),
            pl.BlockSpec((None, SUBLANES, width),
                         lambda b, d, j: (b, jnp.maximum(tile(d, j) * hb - 1, 0), 0)),
            pl.BlockSpec((None, SUBLANES, width),
                         lambda b, d, j: (b, jnp.minimum((tile(d, j) + 1) * hb, L // SUBLANES - 1), 0)),
            pl.BlockSpec(conv_w.shape, lambda b, d, j: (0, 0)),
            pl.BlockSpec((1, width), lambda b, d, j: (0, 0)),
            dir_mat, dir_mat, dir_vec, dir_vec, dir_vec,
        ],
        out_specs=pl.BlockSpec((None, None, tt, width), lambda b, d, j: (d, b, tile(d, j), 0)),
        out_shape=jax.ShapeDtypeStruct((2, bsz, L, width), BF16),
        scratch_shapes=[pltpu.VMEM((tt, width), F32), pltpu.VMEM((tt, width), F32),
                        pltpu.VMEM((tt, width), F32), pltpu.VMEM((SUBLANES, width), F32)],
        compiler_params=_cparams(("arbitrary", "arbitrary", "arbitrary"), 48),
        name="rglru_scan",
    )(xpre, xpre, xpre, conv_w, conv_b, wa, wx, ba, bx, lam)


def _out_tail(h_ref, y, gate_ref, g2_ref, sh2_ref, sc2_ref, wrh_ref, wrl_ref, ho_ref, bo_ref, aff_ref):
    hn = h_ref[...] + gate_ref[...] * y
    ho_ref[...] = hn
    bn = _rms(hn, g2_ref[...]) * (1.0 + sc2_ref[...]) + sh2_ref[...]
    b_hi = bn.astype(BF16)
    bo_ref[...] = b_hi
    b_lo = (bn - b_hi.astype(F32)).astype(BF16)
    wrh = wrh_ref[...]
    logits = _dot_nt(wrh, b_hi) + _dot_nt(wrh, b_lo) + _dot_nt(wrl_ref[...], b_hi)
    ex = jnp.exp(logits - jnp.max(logits, axis=0, keepdims=True))
    aff_ref[...] = ex / jnp.sum(ex, axis=0, keepdims=True)


def _even_out_kernel(h_ref, att_ref, gb_ref, p_ref, pp_ref, pn_ref, cw_ref, cb_ref, w_ref,
                     gate_ref, g2_ref, sh2_ref, sc2_ref, wrh_ref, wrl_ref,
                     ho_ref, bo_ref, aff_ref, *, n_tiles, n_ctx_tiles, attn_w):
    i = pl.program_id(1)
    tm = p_ref.shape[0]
    p = p_ref[...].astype(F32)
    seg_first = jnp.logical_or(i == 0, i == n_ctx_tiles)
    seg_last = jnp.logical_or(i == n_ctx_tiles - 1, i == n_tiles - 1)
    prev = jnp.where(seg_first, 0.0, pp_ref[...].astype(F32)[2 * SUBLANES - 1:2 * SUBLANES, :])
    nxt = jnp.where(seg_last, 0.0, pn_ref[...].astype(F32)[0:1, :])
    row = lax.broadcasted_iota(I32, (tm, 1), 0)
    p_m1 = jnp.where(row == 0, prev, pltpu.roll(p, 1, 0))
    p_p1 = jnp.where(row == tm - 1, nxt, pltpu.roll(p, tm - 1, 0))
    conv = p_m1 * cw_ref[0:1, :] + p * cw_ref[1:2, :] + p_p1 * cw_ref[2:3, :] + cb_ref[...]
    act = (gb_ref[...].astype(F32) * conv).astype(BF16)
    y = _dot(att_ref[...], w_ref[0:attn_w, :]) + _dot(act, w_ref[attn_w:, :])
    _out_tail(h_ref, y, gate_ref, g2_ref, sh2_ref, sc2_ref, wrh_ref, wrl_ref, ho_ref, bo_ref, aff_ref)


def _odd_out_kernel(h_ref, yf_ref, yr_ref, gg_ref, w_ref,
                    gate_ref, g2_ref, sh2_ref, sc2_ref, wrh_ref, wrl_ref,
                    ho_ref, bo_ref, aff_ref):
    ysum = (yf_ref[...].astype(F32) + yr_ref[...].astype(F32)).astype(BF16)
    act = (ysum.astype(F32) * gg_ref[...].astype(F32)).astype(BF16)
    y = _dot(act, w_ref[...])
    _out_tail(h_ref, y, gate_ref, g2_ref, sh2_ref, sc2_ref, wrh_ref, wrl_ref, ho_ref, bo_ref, aff_ref)


def _out_common(bsz, L, d, ctx_len, n_exp):
    tm = ROW_TILE
    nct = ctx_len // tm
    tail_specs = [
        _mod_spec(bsz, nct, d, 2), _resident((1, d)), _mod_spec(bsz, nct, d, 3), _mod_spec(bsz, nct, d, 4),
        _resident((n_exp, d)), _resident((n_exp, d)),
    ]
    out_specs = [
        pl.BlockSpec((None, tm, d), lambda b, i: (b, i, 0)),
        pl.BlockSpec((None, tm, d), lambda b, i: (b, i, 0)),
        pl.BlockSpec((None, n_exp, tm), lambda b, i: (b, 0, i)),
    ]
    out_shape = [
        jax.ShapeDtypeStruct((bsz, L, d), F32),
        jax.ShapeDtypeStruct((bsz, L, d), BF16),
        jax.ShapeDtypeStruct((bsz, n_exp, L), F32),
    ]
    return tm, nct, tail_specs, out_specs, out_shape


def _even_out(h, att, gb, p, conv_w, conv_b, w_out, mods3, g2, wr_hi, wr_lo, ctx_len):
    bsz, L, d = h.shape
    n_exp = wr_hi.shape[0]
    tm, nct, tail_specs, out_specs, out_shape = _out_common(bsz, L, d, ctx_len, n_exp)
    attn_w = att.shape[2]
    conv_wd = p.shape[2]
    halo = 2 * SUBLANES
    hb = tm // halo
    tok = lambda width: pl.BlockSpec((None, tm, width), lambda b, i: (b, i, 0))
    kern = functools.partial(_even_out_kernel, n_tiles=L // tm, n_ctx_tiles=nct, attn_w=attn_w)
    return pl.pallas_call(
        kern,
        grid=(bsz, L // tm),
        in_specs=[
            tok(d), tok(attn_w), tok(conv_wd), tok(conv_wd),
            pl.BlockSpec((None, halo, conv_wd), lambda b, i: (b, jnp.maximum(i * hb - 1, 0), 0)),
            pl.BlockSpec((None, halo, conv_wd), lambda b, i: (b, jnp.minimum((i + 1) * hb, L // halo - 1), 0)),
            _resident(conv_w.shape), _resident((1, conv_wd)), _resident(w_out.shape),
        ] + tail_specs,
        out_specs=out_specs,
        out_shape=out_shape,
        compiler_params=_cparams(("arbitrary", "arbitrary"), 48),
        name="even_out_proj",
    )(h, att, gb, p, p, p, conv_w, conv_b, w_out, mods3, g2, mods3, mods3, wr_hi, wr_lo)


def _odd_out(h, y2, gg, w_out, mods3, g2, wr_hi, wr_lo, ctx_len):
    bsz, L, d = h.shape
    n_exp = wr_hi.shape[0]
    width = gg.shape[2]
    tm, nct, tail_specs, out_specs, out_shape = _out_common(bsz, L, d, ctx_len, n_exp)
    tok = lambda wd: pl.BlockSpec((None, tm, wd), lambda b, i: (b, i, 0))
    return pl.pallas_call(
        _odd_out_kernel,
        grid=(bsz, L // tm),
        in_specs=[
            tok(d),
            pl.BlockSpec((None, None, tm, width), lambda b, i: (0, b, i, 0)),
            pl.BlockSpec((None, None, tm, width), lambda b, i: (1, b, i, 0)),
            tok(width), _resident(w_out.shape),
        ] + tail_specs,
        out_specs=out_specs,
        out_shape=out_shape,
        compiler_params=_cparams(("arbitrary", "arbitrary"), 48),
        name="odd_out_proj",
    )(h, y2, y2, gg, w_out, mods3, g2, mods3, mods3, wr_hi, wr_lo)


def _route_kernel(aff_ref, key_ref, base_ref, *, segments, scatter_tile):
    n_exp, L = aff_ref.shape
    for (start, n, cap, slot_off) in segments:
        if cap == 0:
            key_ref[:, start:start + n] = jnp.full((n_exp, n), -1, I32)
            continue
        aff = aff_ref[:, start:start + n]
        bits = pltpu.bitcast(aff, I32)
        capf = jnp.float32(cap)

        def count(mask):
            return jnp.sum(jnp.where(mask, 1.0, 0.0), axis=1, keepdims=True)

        thr = jnp.zeros((n_exp, 1), I32)
        for bit in range(30, -1, -1):
            cand = thr | jnp.int32(1 << bit)
            thr = jnp.where(count(bits >= cand) >= capf, cand, thr)
        gt = bits > thr
        eq = bits == thr
        need = capf - count(gt)

        cwid = 256 if n % 256 == 0 else LANES
        ri = lax.broadcasted_iota(I32, (cwid, cwid), 0)
        ci = lax.broadcasted_iota(I32, (cwid, cwid), 1)
        upper = jnp.where(ri <= ci, 1.0, 0.0).astype(BF16)

        def cumsum_excl(xf):
            outs = []
            carry = jnp.zeros((n_exp, 1), F32)
            for c in range(n // cwid):
                xc = xf[:, c * cwid:(c + 1) * cwid]
                inc = _dot(xc.astype(BF16), upper)
                outs.append(inc - xc + carry)
                carry = carry + inc[:, cwid - 1:cwid]
            return jnp.concatenate(outs, axis=1)

        eqf = jnp.where(eq, 1.0, 0.0)
        sel = jnp.logical_or(gt, jnp.logical_and(eq, cumsum_excl(eqf) < need))
        self_f = jnp.where(sel, 1.0, 0.0)
        slot = cumsum_excl(self_f) + jnp.float32(slot_off)
        key_ref[:, start:start + n] = jnp.where(sel, slot, -1.0).astype(I32)

    taken = jnp.where(key_ref[...] >= 0, 1.0, 0.0)
    tok = lax.broadcasted_iota(I32, (n_exp, L), 1)
    lane = lax.broadcasted_iota(I32, (n_exp, LANES), 1)
    base = jnp.zeros((n_exp, LANES), F32)
    for k in range(1, L // scatter_tile + 1):
        cnt = jnp.sum(jnp.where(tok < k * scatter_tile, taken, 0.0), axis=1, keepdims=True)
        base = jnp.where(lane == k, cnt, base)
    base_ref[...] = base.astype(I32)


def _route(aff_t, segments, scatter_tile):
    bsz, n_exp, L = aff_t.shape
    return pl.pallas_call(
        functools.partial(_route_kernel, segments=segments, scatter_tile=scatter_tile),
        grid=(bsz,),
        in_specs=[pl.BlockSpec((None, n_exp, L), lambda b: (b, 0, 0))],
        out_specs=[pl.BlockSpec((None, n_exp, L), lambda b: (b, 0, 0)),
                   pl.BlockSpec((None, n_exp, LANES), lambda b: (b, 0, 0))],
        out_shape=[jax.ShapeDtypeStruct((bsz, n_exp, L), I32),
                   jax.ShapeDtypeStruct((bsz, n_exp, LANES), I32)],
        compiler_params=_cparams(("arbitrary",), 32),
        name="moe_route",
    )(aff_t)


def _gather_kernel(key_ref, x_ref, o_ref, *, n_slots):
    e = pl.program_id(1)
    krow = key_ref[pl.ds(e, 1), :]
    slot = lax.broadcasted_iota(I32, (n_slots, 1), 0)
    onehot = jnp.where(krow == slot, 1.0, 0.0).astype(BF16)
    o_ref[...] = _dot(onehot, x_ref[...]).astype(BF16)


def _gather(key, b_act, n_slots):
    bsz, n_exp, L = key.shape
    d = b_act.shape[2]
    return pl.pallas_call(
        functools.partial(_gather_kernel, n_slots=n_slots),
        grid=(bsz, n_exp),
        in_specs=[
            pl.BlockSpec((None, n_exp, L), lambda b, e: (b, 0, 0)),
            pl.BlockSpec((None, L, d), lambda b, e: (b, 0, 0)),
        ],
        out_specs=pl.BlockSpec((None, None, n_slots, d), lambda b, e: (e, b, 0, 0)),
        out_shape=jax.ShapeDtypeStruct((n_exp, bsz, n_slots, d), BF16),
        compiler_params=_cparams(("arbitrary", "arbitrary"), 56),
        name="moe_gather",
    )(key, b_act)


def _expert_up_kernel(x_ref, wg_ref, wu_ref, o_ref):
    x = x_ref[...]
    hg = _dot(x, wg_ref[...].astype(BF16))
    hu = _dot(x, wu_ref[...].astype(BF16))
    o_ref[...] = (hg * _sigmoid(hg) * hu).astype(BF16)


def _expert_down_kernel(h_ref, wd_ref, o_ref):
    o_ref[...] = _dot(h_ref[...], wd_ref[...].astype(BF16)).astype(BF16)


EXPERT_UP_COLS = 256
EXPERT_DOWN_COLS = 512


def _experts(xs, w_gate, w_up, w_down, layer):
    n_exp, rows, d = xs.shape
    ff = w_gate.shape[3]
    tf = EXPERT_UP_COLS
    hid = pl.pallas_call(
        _expert_up_kernel,
        grid=(n_exp, ff // tf),
        in_specs=[
            pl.BlockSpec((None, rows, d), lambda e, f: (e, 0, 0)),
            pl.BlockSpec((None, None, d, tf), lambda e, f: (layer, e, 0, f)),
            pl.BlockSpec((None, None, d, tf), lambda e, f: (layer, e, 0, f)),
        ],
        out_specs=pl.BlockSpec((None, rows, tf), lambda e, f: (e, 0, f)),
        out_shape=jax.ShapeDtypeStruct((n_exp, rows, ff), BF16),
        compiler_params=_cparams(("arbitrary", "arbitrary"), 48),
        name="moe_expert_up",
    )(xs, w_gate, w_up)
    tn = EXPERT_DOWN_COLS
    return pl.pallas_call(
        _expert_down_kernel,
        grid=(n_exp, d // tn),
        in_specs=[
            pl.BlockSpec((None, rows, ff), lambda e, n: (e, 0, 0)),
            pl.BlockSpec((None, None, ff, tn), lambda e, n: (layer, e, 0, n)),
        ],
        out_specs=pl.BlockSpec((None, rows, tn), lambda e, n: (e, 0, n)),
        out_shape=jax.ShapeDtypeStruct((n_exp, rows, d), BF16),
        compiler_params=_cparams(("arbitrary", "arbitrary"), 48),
        name="moe_expert_down",
    )(hid, w_down)


SCATTER_WINDOW = 256
SCATTER_EXPERTS = 2
SCATTER_TILES = 4


def _scatter_kernel(base_ref, keyt_ref, afft_ref, ys_ref, h_ref, gl_ref, gc_ref, gf_ref, o_ref,
                    *, n_slots, ctx_len, tm, n_tiles, final_norm):
    b = pl.program_id(0)
    t = pl.program_id(1)
    eg = pl.program_id(2)
    n_exp = keyt_ref.shape[1]
    win = min(SCATTER_WINDOW, n_slots)
    lane = lax.broadcasted_iota(I32, (tm, n_exp), 1)
    keyt = keyt_ref[...].astype(F32)
    afft = afft_ref[...]
    total = None
    cols = []
    for k in range(SCATTER_EXPERTS):
        e = eg * SCATTER_EXPERTS + k
        pick = lane == e
        kcol = jnp.sum(jnp.where(pick, keyt, 0.0), axis=1, keepdims=True)
        gcol = jnp.sum(jnp.where(pick, afft, 0.0), axis=1, keepdims=True)
        lo = base_ref[(b * (n_tiles + 1) + t) * n_exp + e]
        hi = base_ref[(b * (n_tiles + 1) + t + 1) * n_exp + e]
        start = jnp.minimum((lo // (2 * SUBLANES)) * (2 * SUBLANES), n_slots - win)
        start = pl.multiple_of(start, 2 * SUBLANES)
        slot = (start + lax.broadcasted_iota(I32, (1, win), 1)).astype(F32)
        weights = jnp.where(kcol == slot, gcol, 0.0).astype(BF16)
        part = _dot(weights, ys_ref[k, pl.ds(start, win), :])
        total = part if total is None else total + part
        cols.append((kcol, gcol, start, hi))

    @pl.when(eg == 0)
    def _first():
        o_ref[...] = total

    @pl.when(eg > 0)
    def _rest():
        o_ref[...] += total

    for k, (kcol, gcol, start, hi) in enumerate(cols):
        @pl.when(hi > start + win)
        def _beyond_window():
            slot = lax.broadcasted_iota(I32, (1, n_slots), 1)
            outside = jnp.logical_or(slot < start, slot >= start + win)
            hit = jnp.logical_and(kcol == slot.astype(F32), outside)
            weights = jnp.where(hit, gcol, 0.0).astype(BF16)
            o_ref[...] += _dot(weights, ys_ref[k])

    @pl.when(eg == pl.num_programs(2) - 1)
    def _store():
        row = t * tm + lax.broadcasted_iota(I32, (tm, 1), 0)
        gate = jnp.where(row < ctx_len, gc_ref[...], gl_ref[...])
        hn = h_ref[...] + gate * o_ref[...]
        if final_norm:
            hn = _rms(hn, gf_ref[...])
        o_ref[...] = hn


def _scatter(base, key_t, aff_tt, ys, h, mods3, gfinal, n_slots, ctx_len, final_norm):
    bsz, L, d = h.shape
    n_exp = key_t.shape[2]
    n_tiles = SCATTER_TILES
    tm = L // n_tiles
    assert n_exp % SCATTER_EXPERTS == 0 and (n_slots - min(SCATTER_WINDOW, n_slots)) % (2 * SUBLANES) == 0
    kern = functools.partial(_scatter_kernel, n_slots=n_slots, ctx_len=ctx_len, tm=tm, n_tiles=n_tiles,
                             final_norm=final_norm)
    grid_spec = pltpu.PrefetchScalarGridSpec(
        num_scalar_prefetch=1,
        grid=(bsz, n_tiles, n_exp // SCATTER_EXPERTS),
        in_specs=[
            pl.BlockSpec((None, tm, n_exp), lambda b, t, e, *_: (b, t, 0)),
            pl.BlockSpec((None, tm, n_exp), lambda b, t, e, *_: (b, t, 0)),
            pl.BlockSpec((SCATTER_EXPERTS, None, n_slots, d), lambda b, t, e, *_: (e, b, 0, 0)),
            pl.BlockSpec((None, tm, d), lambda b, t, e, *_: (b, t, 0), pipeline_mode=pl.Buffered(1)),
            pl.BlockSpec((None, 1, d), lambda b, t, e, *_: (b, 0, 5)),
            pl.BlockSpec((None, 1, d), lambda b, t, e, *_: (bsz, 0, 5)),
            pl.BlockSpec((1, d), lambda b, t, e, *_: (0, 0)),
        ],
        out_specs=pl.BlockSpec((None, tm, d), lambda b, t, e, *_: (b, t, 0)),
    )
    return pl.pallas_call(
        kern,
        grid_spec=grid_spec,
        out_shape=jax.ShapeDtypeStruct((bsz, L, d), F32),
        compiler_params=_cparams(("arbitrary", "arbitrary", "arbitrary"), 56),
        name="moe_scatter",
    )(base, key_t, aff_tt, ys, h, mods3, mods3, gfinal)


def _rope_tables(ctx_len, seq):
    pos = np.arange(seq)
    freqs = ROPE_THETA ** (-jnp.arange(0, ROT_AXIS, 2, dtype=F32) / ROT_AXIS)

    def cs(p):
        ang = jnp.asarray(p, F32)[:, None] * freqs[None, :]
        return jnp.cos(ang), jnp.sin(ang)

    cr, sr = cs(pos // GRID_W)
    cc, sc = cs(pos % GRID_W)
    cos_l = jnp.concatenate([cr, cr, cc, cc], axis=1)
    sin_l = jnp.concatenate([-sr, sr, -sc, sc], axis=1)
    cos_t = jnp.concatenate([jnp.ones((ctx_len, HEAD_DIM), F32), cos_l], axis=0)
    sin_t = jnp.concatenate([jnp.zeros((ctx_len, HEAD_DIM), F32), sin_l], axis=0)
    return cos_t, sin_t


def _moe(h, b_act, aff_t, mods3, w_gate, w_up, w_down, layer, gfinal, ctx_len, moe_ctx, final_norm):
    bsz, L, d = h.shape
    seq = L - ctx_len
    n_exp = aff_t.shape[1]
    cap_l = max(1, EC_FACTOR * seq // n_exp)
    cap_c = max(1, EC_FACTOR * ctx_len // n_exp) if moe_ctx else 0
    n_slots = cap_l + cap_c
    segments = ((0, ctx_len, cap_c, 0), (ctx_len, seq, cap_l, cap_c))
    key, base = _route(aff_t, segments, L // SCATTER_TILES)
    xs = _gather(key, b_act, n_slots)
    ys = _experts(xs.reshape(n_exp, bsz * n_slots, d), w_gate, w_up, w_down, layer)
    ys = ys.reshape(n_exp, bsz, n_slots, d)
    base_flat = base[:, :, :SCATTER_TILES + 1].transpose(0, 2, 1).reshape(-1)
    return _scatter(base_flat, key.transpose(0, 2, 1), aff_t.transpose(0, 2, 1), ys, h, mods3, gfinal,
                    n_slots, ctx_len, final_norm)


def kernel(x, c, ctx, c_ctx, mod_w, mod_b, norm1_g, norm2_g, ev_w_in, ev_q_norm, ev_k_norm, ev_conv_w, ev_conv_b, ev_w_out, od_w_in, od_conv_w, od_conv_b, od_wa, od_ba, od_wx, od_bx, od_lam, od_w_out, moe_router, moe_w_gate, moe_w_up, moe_w_down, final_norm_g):
    bsz, seq, d = x.shape
    ctx_len = ctx.shape[1]
    depth = mod_w.shape[0]
    assert bsz < SUBLANES and ctx_len % ROW_TILE == 0 and seq % (2 * KV_CHUNK) == 0
    attn_w = d // 2
    kv_w = attn_w // Q_PER_KV
    conv_w = d // 2
    lru_w = od_w_in.shape[2] // 2

    h = jnp.concatenate([ctx, x], axis=1)
    cvec = jnp.concatenate([c, c_ctx[None, :], jnp.zeros((SUBLANES - bsz - 1, d), F32)], axis=0)
    mods = _mods(cvec, mod_w, mod_b)
    cos_t, sin_t = _rope_tables(ctx_len, seq)
    gfinal = final_norm_g.reshape(1, d)

    for i in range(depth):
        last = i == depth - 1
        mods3 = mods[i].reshape(SUBLANES, 1, 6 * d)
        g1 = norm1_g[i].reshape(1, d)
        g2 = norm2_g[i].reshape(1, d)
        wr = moe_router[i].T
        wr_hi = wr.astype(BF16)
        wr_lo = (wr - wr_hi.astype(F32)).astype(BF16)
        j = i // 2
        if i % 2 == 0:
            q, k, vt, gb, p = _even_in(h, mods3, g1, ev_w_in[j].astype(BF16),
                                      ev_q_norm[j].reshape(1, HEAD_DIM), ev_k_norm[j].reshape(1, HEAD_DIM),
                                      cos_t, sin_t, ctx_len, attn_w, kv_w, conv_w)
            att = _attention(q, k, vt, ctx_len)
            h, b_act, aff_t = _even_out(h, att, gb, p, ev_conv_w[j], ev_conv_b[j].reshape(1, conv_w),
                                        ev_w_out[j].astype(BF16), mods3, g2, wr_hi, wr_lo, ctx_len)
        else:
            xpre, gg = _odd_in(h, mods3, g1, od_w_in[j].astype(BF16), ctx_len, lru_w)
            y2 = _lru(xpre, od_conv_w[j], od_conv_b[j].reshape(1, lru_w),
                      od_wa[j].astype(BF16), od_wx[j].astype(BF16),
                      od_ba[j].reshape(2, 1, lru_w), od_bx[j].reshape(2, 1, lru_w),
                      od_lam[j].reshape(2, 1, lru_w), ctx_len)
            h, b_act, aff_t = _odd_out(h, y2, gg, od_w_out[j].astype(BF16), mods3, g2, wr_hi, wr_lo, ctx_len)
        h = _moe(h, b_act, aff_t, mods3, moe_w_gate, moe_w_up, moe_w_down, i, gfinal,
                 ctx_len, moe_ctx=not last, final_norm=last)
    return h[:, ctx_len:, :]
```

```python
import functools

import jax
import jax.numpy as jnp
import numpy as np
from jax import lax
from jax.experimental import pallas as pl
from jax.experimental.pallas import tpu as pltpu

F32 = jnp.float32
BF16 = jnp.bfloat16
I32 = jnp.int32
U32 = jnp.uint32

EPS = 1e-6
HEAD_DIM = 128
ROT_AXIS = HEAD_DIM // 2
ROPE_THETA = 10000.0
GRID_W = 64
Q_PER_KV = 4
LRU_BLOCK = 256
LRU_C = 8.0
N_EXPERTS = 16
EC_FACTOR = 2

LANES = 128
SUBLANES = 8
ROW_TILE = 256
KV_CHUNK = 512
MIB = 1024 * 1024


def _cparams(sem, vmem_mib):
    return pltpu.CompilerParams(dimension_semantics=sem, vmem_limit_bytes=int(vmem_mib * MIB))


def _sigmoid(x):
    return 1.0 / (1.0 + jnp.exp(-x))


def _rms(x, g):
    return x * lax.rsqrt(jnp.mean(x * x, axis=-1, keepdims=True) + EPS) * g


def _dot(a, b):
    return jnp.dot(a, b, preferred_element_type=F32)


def _dot_nt(a, b):
    return lax.dot_general(a, b, (((1,), (1,)), ((), ())), preferred_element_type=F32)


def _resident(shape):
    nd = len(shape)
    return pl.BlockSpec(shape, lambda *_: (0,) * nd)


def _mods_kernel(c_ref, w_ref, b_ref, o_ref):
    c = c_ref[...]
    s = (c * _sigmoid(c)).astype(BF16)
    o_ref[...] = _dot(s, w_ref[...].astype(BF16)) + b_ref[...]


def _mods(cvec, mod_w, mod_b):
    depth, d, n6 = mod_w.shape
    tn = 1536
    return pl.pallas_call(
        _mods_kernel,
        grid=(depth, n6 // tn),
        in_specs=[
            pl.BlockSpec((SUBLANES, d), lambda l, j: (0, 0)),
            pl.BlockSpec((None, d, tn), lambda l, j: (l, 0, j)),
            pl.BlockSpec((None, 1, tn), lambda l, j: (l, 0, j)),
        ],
        out_specs=pl.BlockSpec((None, SUBLANES, tn), lambda l, j: (l, 0, j)),
        out_shape=jax.ShapeDtypeStruct((depth, SUBLANES, n6), F32),
        compiler_params=_cparams(("arbitrary", "arbitrary"), 40),
        name="adaln_mods",
    )(cvec, mod_w, mod_b.reshape(depth, 1, n6))


def _mod_spec(bsz, n_ctx_tiles, d, k):
    return pl.BlockSpec((None, 1, d), lambda b, i: (jnp.where(i < n_ctx_tiles, bsz, b), 0, k))


def _even_in_kernel(h_ref, g1_ref, sh_ref, sc_ref, w_ref, qg_ref, kg_ref, cos_ref, sin_ref,
                    q_ref, k_ref, v_ref, gb_ref, p_ref, *, attn_w, kv_w, conv_w):
    x = h_ref[...]
    a = (_rms(x, g1_ref[...]) * (1.0 + sc_ref[...]) + sh_ref[...]).astype(BF16)
    cos = cos_ref[...]
    sin = sin_ref[...]
    tm = x.shape[0]
    lane = lax.broadcasted_iota(I32, (tm, HEAD_DIM), 1)
    low_half = (lane % ROT_AXIS) < (ROT_AXIS // 2)

    def norm_rope(t, g, scale):
        t = t * lax.rsqrt(jnp.mean(t * t, axis=-1, keepdims=True) + EPS) * g
        swapped = jnp.where(low_half, pltpu.roll(t, HEAD_DIM - ROT_AXIS // 2, 1),
                            pltpu.roll(t, ROT_AXIS // 2, 1))
        return (t * cos + swapped * sin) * scale

    cw = 4 * HEAD_DIM
    qg = qg_ref[...]
    kg = kg_ref[...]
    q_scale = HEAD_DIM ** -0.5 * np.log2(np.e)
    for c0 in range(0, attn_w, cw):
        y = _dot(a, w_ref[:, c0:c0 + cw])
        for j in range(cw // HEAD_DIM):
            t = norm_rope(y[:, j * HEAD_DIM:(j + 1) * HEAD_DIM], qg, q_scale)
            q_ref[:, c0 + j * HEAD_DIM:c0 + (j + 1) * HEAD_DIM] = t.astype(BF16)
    y = _dot(a, w_ref[:, attn_w:attn_w + 2 * kv_w])
    for j in range(kv_w // HEAD_DIM):
        t = norm_rope(y[:, j * HEAD_DIM:(j + 1) * HEAD_DIM], kg, 1.0)
        k_ref[:, j * HEAD_DIM:(j + 1) * HEAD_DIM] = t.astype(BF16)
    v_ref[...] = y[:, kv_w:2 * kv_w].T.astype(BF16)
    off_b = attn_w + 2 * kv_w
    off_c = off_b + conv_w
    off_u = off_c + conv_w
    for c0 in range(0, conv_w, cw):
        gb_ref[:, c0:c0 + cw] = _dot(a, w_ref[:, off_b + c0:off_b + c0 + cw]).astype(BF16)
        cc = _dot(a, w_ref[:, off_c + c0:off_c + c0 + cw])
        uu = _dot(a, w_ref[:, off_u + c0:off_u + c0 + cw])
        p_ref[:, c0:c0 + cw] = (cc * uu).astype(BF16)


def _even_in(h, mods3, g1, w_in, qg, kg, cos_t, sin_t, ctx_len, attn_w, kv_w, conv_w):
    bsz, L, d = h.shape
    tm = ROW_TILE
    nct = ctx_len // tm
    tok = lambda width: pl.BlockSpec((None, tm, width), lambda b, i: (b, i, 0))
    kern = functools.partial(_even_in_kernel, attn_w=attn_w, kv_w=kv_w, conv_w=conv_w)
    return pl.pallas_call(
        kern,
        grid=(bsz, L // tm),
        in_specs=[
            tok(d),
            _resident((1, d)),
            _mod_spec(bsz, nct, d, 0),
            _mod_spec(bsz, nct, d, 1),
            _resident(w_in.shape),
            _resident((1, HEAD_DIM)),
            _resident((1, HEAD_DIM)),
            pl.BlockSpec((tm, HEAD_DIM), lambda b, i: (i, 0)),
            pl.BlockSpec((tm, HEAD_DIM), lambda b, i: (i, 0)),
        ],
        out_specs=[tok(attn_w), tok(kv_w), pl.BlockSpec((None, kv_w, tm), lambda b, i: (b, 0, i)),
                   tok(conv_w), tok(conv_w)],
        out_shape=[
            jax.ShapeDtypeStruct((bsz, L, attn_w), BF16),
            jax.ShapeDtypeStruct((bsz, L, kv_w), BF16),
            jax.ShapeDtypeStruct((bsz, kv_w, L), BF16),
            jax.ShapeDtypeStruct((bsz, L, conv_w), BF16),
            jax.ShapeDtypeStruct((bsz, L, conv_w), BF16),
        ],
        compiler_params=_cparams(("arbitrary", "arbitrary"), 56),
        name="even_in_proj",
    )(h, g1, mods3, mods3, w_in, qg, kg, cos_t, sin_t)


def _attn_kernel(q_ref, k_ref, vt_ref, o_ref, m_sc, l_sc, acc_sc, sa_sc, sb_sc, *, ctx_len, n_lat_chunks, tq):
    qi = pl.program_id(2)
    q = q_ref[...]
    qs = jnp.concatenate([q[:, j * HEAD_DIM:(j + 1) * HEAD_DIM] for j in range(Q_PER_KV)], axis=0)

    def lat_start(c):
        return pl.multiple_of(ctx_len + c * KV_CHUNK, KV_CHUNK // 2)

    def scores(c):
        return _dot_nt(k_ref[pl.ds(lat_start(c), KV_CHUNK), :], qs)

    def absorb(s_ref, c):
        sc = s_ref[...]
        m_old = m_sc[...]
        m_new = jnp.maximum(m_old, jnp.max(sc, axis=0, keepdims=True))
        alpha = jnp.exp2(m_old - m_new)
        pc = jnp.exp2(sc - m_new)
        l_sc[...] = alpha * l_sc[...] + jnp.sum(pc, axis=0, keepdims=True)
        acc_sc[...] = alpha * acc_sc[...] + _dot(vt_ref[:, pl.ds(lat_start(c), KV_CHUNK)], pc.astype(BF16))
        m_sc[...] = m_new

    s = _dot_nt(k_ref[0:ctx_len, :], qs)
    m0 = jnp.max(s, axis=0, keepdims=True)
    p = jnp.exp2(s - m0)
    m_sc[...] = m0
    l_sc[...] = jnp.sum(p, axis=0, keepdims=True)
    acc_sc[...] = _dot(vt_ref[:, 0:ctx_len], p.astype(BF16))

    @pl.when(qi >= ctx_len // tq)
    def _latent_keys():
        sa_sc[...] = scores(0)

        def body(i, carry):
            c = 2 * i
            sb_sc[...] = scores(c + 1)
            absorb(sa_sc, c)
            sa_sc[...] = scores(jnp.minimum(c + 2, n_lat_chunks - 1))
            absorb(sb_sc, c + 1)
            return carry
        lax.fori_loop(0, n_lat_chunks // 2, body, 0)

    o = acc_sc[...] * (1.0 / l_sc[...])
    for j in range(Q_PER_KV):
        o_ref[:, j * HEAD_DIM:(j + 1) * HEAD_DIM] = o[:, j * tq:(j + 1) * tq].T.astype(BF16)


def _attention(q, k, vt, ctx_len):
    bsz, L, attn_w = q.shape
    n_kv = k.shape[2] // HEAD_DIM
    tq = ROW_TILE
    gw = Q_PER_KV * HEAD_DIM
    seq = L - ctx_len
    kern = functools.partial(_attn_kernel, ctx_len=ctx_len, n_lat_chunks=seq // KV_CHUNK, tq=tq)
    return pl.pallas_call(
        kern,
        grid=(bsz, n_kv, L // tq),
        in_specs=[
            pl.BlockSpec((None, tq, gw), lambda b, g, i: (b, i, g)),
            pl.BlockSpec((None, L, HEAD_DIM), lambda b, g, i: (b, 0, g)),
            pl.BlockSpec((None, HEAD_DIM, L), lambda b, g, i: (b, g, 0)),
        ],
        out_specs=pl.BlockSpec((None, tq, gw), lambda b, g, i: (b, i, g)),
        out_shape=jax.ShapeDtypeStruct((bsz, L, attn_w), BF16),
        scratch_shapes=[
            pltpu.VMEM((1, Q_PER_KV * tq), F32),
            pltpu.VMEM((1, Q_PER_KV * tq), F32),
            pltpu.VMEM((HEAD_DIM, Q_PER_KV * tq), F32),
            pltpu.VMEM((KV_CHUNK, Q_PER_KV * tq), F32),
            pltpu.VMEM((KV_CHUNK, Q_PER_KV * tq), F32),
        ],
        compiler_params=_cparams(("arbitrary", "arbitrary", "arbitrary"), 40),
        name="attention",
    )(q, k, vt)


def _gelu_tanh(x):
    return 0.5 * x * (1.0 + jnp.tanh(np.sqrt(2.0 / np.pi) * (x + 0.044715 * (x * x * x))))


def _odd_in_kernel(h_ref, g1_ref, sh_ref, sc_ref, w_ref, x_ref, gg_ref, *, lru_w):
    x = h_ref[...]
    a = (_rms(x, g1_ref[...]) * (1.0 + sc_ref[...]) + sh_ref[...]).astype(BF16)
    cw = 512
    for c0 in range(0, lru_w, cw):
        x_ref[:, c0:c0 + cw] = _dot(a, w_ref[:, c0:c0 + cw])
        gg_ref[:, c0:c0 + cw] = _gelu_tanh(_dot(a, w_ref[:, lru_w + c0:lru_w + c0 + cw])).astype(BF16)


def _odd_in(h, mods3, g1, w_in, ctx_len, lru_w):
    bsz, L, d = h.shape
    tm = ROW_TILE
    nct = ctx_len // tm
    tok = lambda width: pl.BlockSpec((None, tm, width), lambda b, i: (b, i, 0))
    return pl.pallas_call(
        functools.partial(_odd_in_kernel, lru_w=lru_w),
        grid=(bsz, L // tm),
        in_specs=[tok(d), _resident((1, d)), _mod_spec(bsz, nct, d, 0), _mod_spec(bsz, nct, d, 1),
                  _resident(w_in.shape)],
        out_specs=[tok(lru_w), tok(lru_w)],
        out_shape=[jax.ShapeDtypeStruct((bsz, L, lru_w), F32),
                   jax.ShapeDtypeStruct((bsz, L, lru_w), BF16)],
        compiler_params=_cparams(("arbitrary", "arbitrary"), 56),
        name="odd_in_proj",
    )(h, g1, mods3, mods3, w_in)


def _lru_tile(d, j, n_tiles, n_ctx_tiles):
    rev = jnp.where(j < n_ctx_tiles, n_ctx_tiles - 1 - j, n_tiles - 1 - (j - n_ctx_tiles))
    return jnp.where(d == 0, j, rev)


def _lru_kernel(x_ref, xp_ref, xn_ref, cw_ref, cb_ref, wa_ref, wx_ref, ba_ref, bx_ref, lam_ref,
                y_ref, a_sc, b_sc, y_sc, h_sc, *, n_tiles, n_ctx_tiles, tt):
    d = pl.program_id(1)
    j = pl.program_id(2)
    tile = _lru_tile(d, j, n_tiles, n_ctx_tiles)
    width = x_ref.shape[1]
    x = x_ref[...]
    seg_first = jnp.logical_or(tile == 0, tile == n_ctx_tiles)
    seg_last = jnp.logical_or(tile == n_ctx_tiles - 1, tile == n_tiles - 1)
    prev = jnp.where(seg_first, 0.0, xp_ref[SUBLANES - 1:SUBLANES, :])
    nxt1 = jnp.where(seg_last, 0.0, xn_ref[0:1, :])
    nxt2 = jnp.where(seg_last, 0.0, xn_ref[1:2, :])
    w0, w1, w2, w3 = (cw_ref[k:k + 1, :] for k in range(4))
    u = (pltpu.roll(x, 1, 0) * w0 + x * w1 + pltpu.roll(x, tt - 1, 0) * w2 + pltpu.roll(x, tt - 2, 0) * w3
         + cb_ref[...])
    sub8 = lax.broadcasted_iota(I32, (SUBLANES, 1), 0)
    fix_first = jnp.where(sub8 == 0, (prev - x[tt - 1:tt, :]) * w0, 0.0)
    fix_last = jnp.where(sub8 == SUBLANES - 1, (nxt1 - x[0:1, :]) * w2 + (nxt2 - x[1:2, :]) * w3,
                         jnp.where(sub8 == SUBLANES - 2, (nxt1 - x[0:1, :]) * w3, 0.0))
    u = jnp.concatenate([u[0:SUBLANES] + fix_first, u[SUBLANES:tt - SUBLANES], u[tt - SUBLANES:] + fix_last],
                        axis=0)
    z = -lam_ref[...]
    la = (-LRU_C * np.log2(np.e)) * (jnp.maximum(z, 0.0) + jnp.log(1.0 + jnp.exp(-jnp.abs(z))))
    for blk in range(width // LRU_BLOCK):
        cs = slice(blk * LRU_BLOCK, (blk + 1) * LRU_BLOCK)
        ub = u[:, cs]
        ub16 = ub.astype(BF16)
        r = _sigmoid(_dot(ub16, wa_ref[blk]) + ba_ref[:, cs])
        ig = _sigmoid(_dot(ub16, wx_ref[blk]) + bx_ref[:, cs])
        a = jnp.exp2(r * la[:, cs])
        a_sc[:, cs] = a
        b_sc[:, cs] = jnp.sqrt(1.0 - a * a) * ig * ub

    @pl.when(j == 0)
    def _reset():
        h_sc[...] = jnp.zeros_like(h_sc)

    sub = lax.broadcasted_iota(I32, (SUBLANES, width), 0)
    n_groups = tt // SUBLANES

    def group_scan(base, forward):
        av = a_sc[pl.ds(base, SUBLANES), :]
        bv = b_sc[pl.ds(base, SUBLANES), :]
        for s in (1, 2, 4):
            shift = s if forward else SUBLANES - s
            keep = (sub >= s) if forward else (sub < SUBLANES - s)
            a_sh = pltpu.roll(av, shift, 0)
            b_sh = pltpu.roll(bv, shift, 0)
            bv = jnp.where(keep, av * b_sh + bv, bv)
            av = jnp.where(keep, av * a_sh, av)
        out = av * h_sc[...] + bv
        y_sc[pl.ds(base, SUBLANES), :] = out
        last = out[SUBLANES - 1:SUBLANES, :] if forward else out[0:1, :]
        h_sc[...] = jnp.broadcast_to(last, (SUBLANES, width))

    @pl.when(d == 0)
    def _forward():
        def body(g, carry):
            group_scan(pl.multiple_of(g * SUBLANES, SUBLANES), True)
            return carry
        lax.fori_loop(0, n_groups, body, 0)

    @pl.when(d == 1)
    def _backward():
        def body(g, carry):
            group_scan(pl.multiple_of((n_groups - 1 - g) * SUBLANES, SUBLANES), False)
            return carry
        lax.fori_loop(0, n_groups, body, 0)

    y_ref[...] = y_sc[...].astype(BF16)


def _lru(xpre, conv_w, conv_b, wa, wx, ba, bx, lam, ctx_len):
    bsz, L, width = xpre.shape
    tt = ROW_TILE
    nt = L // tt
    nct = ctx_len // tt
    nb = width // LRU_BLOCK
    hb = tt // SUBLANES
    tile = lambda d, j: _lru_tile(d, j, nt, nct)
    kern = functools.partial(_lru_kernel, n_tiles=nt, n_ctx_tiles=nct, tt=tt)
    dir_vec = pl.BlockSpec((None, 1, width), lambda b, d, j: (d, 0, 0))
    dir_mat = pl.BlockSpec((None, nb, LRU_BLOCK, LRU_BLOCK), lambda b, d, j: (d, 0, 0, 0))
    return pl.pallas_call(
        kern,
        grid=(bsz, 2, nt),
        in_specs=[
            pl.BlockSpec((None, tt, width), lambda b, d, j: (b, tile(d, j), 0)),
            pl.BlockSpec((None, SUBLANES, width),
                         lambda b, d, j: (b, jnp.maximum(tile(d, j) * hb - 1, 0), 0)),
            pl.BlockSpec((None, SUBLANES, width),
                         lambda b, d, j: (b, jnp.minimum((tile(d, j) + 1) * hb, L // SUBLANES - 1), 0)),
            pl.BlockSpec(conv_w.shape, lambda b, d, j: (0, 0)),
            pl.BlockSpec((1, width), lambda b, d, j: (0, 0)),
            dir_mat, dir_mat, dir_vec, dir_vec, dir_vec,
        ],
        out_specs=pl.BlockSpec((None, None, tt, width), lambda b, d, j: (d, b, tile(d, j), 0)),
        out_shape=jax.ShapeDtypeStruct((2, bsz, L, width), BF16),
        scratch_shapes=[pltpu.VMEM((tt, width), F32), pltpu.VMEM((tt, width), F32),
                        pltpu.VMEM((tt, width), F32), pltpu.VMEM((SUBLANES, width), F32)],
        compiler_params=_cparams(("arbitrary", "arbitrary", "arbitrary"), 48),
        name="rglru_scan",
    )(xpre, xpre, xpre, conv_w, conv_b, wa, wx, ba, bx, lam)


def _out_tail(h_ref, y, gate_ref, g2_ref, sh2_ref, sc2_ref, wrh_ref, wrl_ref, ho_ref, bo_ref, aff_ref):
    hn = h_ref[...] + gate_ref[...] * y
    ho_ref[...] = hn
    bn = _rms(hn, g2_ref[...]) * (1.0 + sc2_ref[...]) + sh2_ref[...]
    b_hi = bn.astype(BF16)
    half = bn.shape[1] // 2
    lo = lax.shift_right_logical(pltpu.bitcast(b_hi[:, :half].astype(F32), U32), jnp.uint32(16))
    hi = pltpu.bitcast(b_hi[:, half:].astype(F32), U32) & jnp.uint32(0xFFFF0000)
    bo_ref[...] = lo | hi
    b_lo = (bn - b_hi.astype(F32)).astype(BF16)
    wrh = wrh_ref[...]
    logits = _dot_nt(wrh, b_hi) + _dot_nt(wrh, b_lo) + _dot_nt(wrl_ref[...], b_hi)
    ex = jnp.exp(logits - jnp.max(logits, axis=0, keepdims=True))
    aff_ref[...] = ex / jnp.sum(ex, axis=0, keepdims=True)


def _even_out_kernel(h_ref, att_ref, gb_ref, p_ref, pp_ref, pn_ref, cw_ref, cb_ref, w_ref,
                     gate_ref, g2_ref, sh2_ref, sc2_ref, wrh_ref, wrl_ref,
                     ho_ref, bo_ref, aff_ref, *, n_tiles, n_ctx_tiles, attn_w):
    i = pl.program_id(1)
    tm = p_ref.shape[0]
    p = p_ref[...].astype(F32)
    seg_first = jnp.logical_or(i == 0, i == n_ctx_tiles)
    seg_last = jnp.logical_or(i == n_ctx_tiles - 1, i == n_tiles - 1)
    prev = jnp.where(seg_first, 0.0, pp_ref[...].astype(F32)[2 * SUBLANES - 1:2 * SUBLANES, :])
    nxt = jnp.where(seg_last, 0.0, pn_ref[...].astype(F32)[0:1, :])
    row = lax.broadcasted_iota(I32, (tm, 1), 0)
    p_m1 = jnp.where(row == 0, prev, pltpu.roll(p, 1, 0))
    p_p1 = jnp.where(row == tm - 1, nxt, pltpu.roll(p, tm - 1, 0))
    conv = p_m1 * cw_ref[0:1, :] + p * cw_ref[1:2, :] + p_p1 * cw_ref[2:3, :] + cb_ref[...]
    act = (gb_ref[...].astype(F32) * conv).astype(BF16)
    y = _dot(att_ref[...], w_ref[0:attn_w, :]) + _dot(act, w_ref[attn_w:, :])
    _out_tail(h_ref, y, gate_ref, g2_ref, sh2_ref, sc2_ref, wrh_ref, wrl_ref, ho_ref, bo_ref, aff_ref)


def _odd_out_kernel(h_ref, yf_ref, yr_ref, gg_ref, w_ref,
                    gate_ref, g2_ref, sh2_ref, sc2_ref, wrh_ref, wrl_ref,
                    ho_ref, bo_ref, aff_ref):
    ysum = (yf_ref[...].astype(F32) + yr_ref[...].astype(F32)).astype(BF16)
    act = (ysum.astype(F32) * gg_ref[...].astype(F32)).astype(BF16)
    y = _dot(act, w_ref[...])
    _out_tail(h_ref, y, gate_ref, g2_ref, sh2_ref, sc2_ref, wrh_ref, wrl_ref, ho_ref, bo_ref, aff_ref)


def _out_common(bsz, L, d, ctx_len, n_exp):
    tm = ROW_TILE
    nct = ctx_len // tm
    tail_specs = [
        _mod_spec(bsz, nct, d, 2), _resident((1, d)), _mod_spec(bsz, nct, d, 3), _mod_spec(bsz, nct, d, 4),
        _resident((n_exp, d)), _resident((n_exp, d)),
    ]
    out_specs = [
        pl.BlockSpec((None, tm, d), lambda b, i: (b, i, 0)),
        pl.BlockSpec((None, tm, d // 2), lambda b, i: (b, i, 0)),
        pl.BlockSpec((None, n_exp, tm), lambda b, i: (b, 0, i)),
    ]
    out_shape = [
        jax.ShapeDtypeStruct((bsz, L, d), F32),
        jax.ShapeDtypeStruct((bsz, L, d // 2), U32),
        jax.ShapeDtypeStruct((bsz, n_exp, L), F32),
    ]
    return tm, nct, tail_specs, out_specs, out_shape


def _even_out(h, att, gb, p, conv_w, conv_b, w_out, mods3, g2, wr_hi, wr_lo, ctx_len):
    bsz, L, d = h.shape
    n_exp = wr_hi.shape[0]
    tm, nct, tail_specs, out_specs, out_shape = _out_common(bsz, L, d, ctx_len, n_exp)
    attn_w = att.shape[2]
    conv_wd = p.shape[2]
    halo = 2 * SUBLANES
    hb = tm // halo
    tok = lambda width: pl.BlockSpec((None, tm, width), lambda b, i: (b, i, 0))
    kern = functools.partial(_even_out_kernel, n_tiles=L // tm, n_ctx_tiles=nct, attn_w=attn_w)
    return pl.pallas_call(
        kern,
        grid=(bsz, L // tm),
        in_specs=[
            tok(d), tok(attn_w), tok(conv_wd), tok(conv_wd),
            pl.BlockSpec((None, halo, conv_wd), lambda b, i: (b, jnp.maximum(i * hb - 1, 0), 0)),
            pl.BlockSpec((None, halo, conv_wd), lambda b, i: (b, jnp.minimum((i + 1) * hb, L // halo - 1), 0)),
            _resident(conv_w.shape), _resident((1, conv_wd)), _resident(w_out.shape),
        ] + tail_specs,
        out_specs=out_specs,
        out_shape=out_shape,
        compiler_params=_cparams(("arbitrary", "arbitrary"), 48),
        name="even_out_proj",
    )(h, att, gb, p, p, p, conv_w, conv_b, w_out, mods3, g2, mods3, mods3, wr_hi, wr_lo)


def _odd_out(h, y2, gg, w_out, mods3, g2, wr_hi, wr_lo, ctx_len):
    bsz, L, d = h.shape
    n_exp = wr_hi.shape[0]
    width = gg.shape[2]
    tm, nct, tail_specs, out_specs, out_shape = _out_common(bsz, L, d, ctx_len, n_exp)
    tok = lambda wd: pl.BlockSpec((None, tm, wd), lambda b, i: (b, i, 0))
    return pl.pallas_call(
        _odd_out_kernel,
        grid=(bsz, L // tm),
        in_specs=[
            tok(d),
            pl.BlockSpec((None, None, tm, width), lambda b, i: (0, b, i, 0)),
            pl.BlockSpec((None, None, tm, width), lambda b, i: (1, b, i, 0)),
            tok(width), _resident(w_out.shape),
        ] + tail_specs,
        out_specs=out_specs,
        out_shape=out_shape,
        compiler_params=_cparams(("arbitrary", "arbitrary"), 48),
        name="odd_out_proj",
    )(h, y2, y2, gg, w_out, mods3, g2, mods3, mods3, wr_hi, wr_lo)


def _route_kernel(aff_ref, key_ref, base_ref, tok_ref, lc_sc, *, segments, scatter_tile, n_slots):
    n_exp, L = aff_ref.shape
    for (start, n, cap, slot_off) in segments:
        if cap == 0:
            key_ref[:, start:start + n] = jnp.full((n_exp, n), -1, I32)
            continue
        aff = aff_ref[:, start:start + n]
        bits = pltpu.bitcast(aff, I32)
        capf = jnp.float32(cap)

        def count(mask):
            return jnp.sum(jnp.where(mask, 1.0, 0.0), axis=1, keepdims=True)

        thr = jnp.zeros((n_exp, 1), I32)
        for bit in range(30, -1, -1):
            cand = thr | jnp.int32(1 << bit)
            thr = jnp.where(count(bits >= cand) >= capf, cand, thr)
        gt = bits > thr
        eq = bits == thr
        need = capf - count(gt)

        cwid = 256 if n % 256 == 0 else LANES
        ri = lax.broadcasted_iota(I32, (cwid, cwid), 0)
        ci = lax.broadcasted_iota(I32, (cwid, cwid), 1)
        upper = jnp.where(ri <= ci, 1.0, 0.0).astype(BF16)

        def cumsum_excl(xf):
            outs = []
            carry = jnp.zeros((n_exp, 1), F32)
            for c in range(n // cwid):
                xc = xf[:, c * cwid:(c + 1) * cwid]
                inc = _dot(xc.astype(BF16), upper)
                outs.append(inc - xc + carry)
                carry = carry + inc[:, cwid - 1:cwid]
            return jnp.concatenate(outs, axis=1)

        eqf = jnp.where(eq, 1.0, 0.0)
        sel = jnp.logical_or(gt, jnp.logical_and(eq, cumsum_excl(eqf) < need))
        self_f = jnp.where(sel, 1.0, 0.0)
        slot = cumsum_excl(self_f) + jnp.float32(slot_off)
        key_ref[:, start:start + n] = jnp.where(sel, slot, -1.0).astype(I32)

    taken = jnp.where(key_ref[...] >= 0, 1.0, 0.0)
    tok = lax.broadcasted_iota(I32, (n_exp, L), 1)
    lane = lax.broadcasted_iota(I32, (n_exp, LANES), 1)
    base = jnp.zeros((n_exp, LANES), F32)
    for k in range(1, L // scatter_tile + 1):
        cnt = jnp.sum(jnp.where(tok < k * scatter_tile, taken, 0.0), axis=1, keepdims=True)
        base = jnp.where(lane == k, cnt, base)
    base_ref[...] = base.astype(I32)

    cw = ROUTE_CHUNK
    nch = L // cw
    ri = lax.broadcasted_iota(I32, (cw, cw), 0)
    ci = lax.broadcasted_iota(I32, (cw, cw), 1)
    upper = jnp.where(ri <= ci, 1.0, 0.0).astype(BF16)
    incs = []
    big = jnp.float32(2 ** 30)
    chunk_lo = jnp.full((n_exp, LANES), big, F32)
    chunk_hi = jnp.full((n_exp, LANES), big, F32)
    run = jnp.zeros((n_exp, 1), F32)
    for j in range(nch):
        inc = _dot(taken[:, j * cw:(j + 1) * cw].astype(BF16), upper)
        incs.append(inc)
        chunk_lo = jnp.where(lane == j, run, chunk_lo)
        run = run + inc[:, cw - 1:cw]
        chunk_hi = jnp.where(lane == j, run, chunk_hi)
    lc_sc[...] = jnp.zeros_like(lc_sc)
    slot_col = lax.broadcasted_iota(I32, (n_slots, 1), 0).astype(F32)
    lane_s = lax.broadcasted_iota(I32, (n_slots, LANES), 1)
    lane_sf = lane_s.astype(F32)
    toks = jnp.zeros((n_slots, LANES), F32)
    for e in range(n_exp):
        for j in range(nch):
            lc_sc[j:j + 1, :] = incs[j][e:e + 1, :]
        lo_e = chunk_lo[e:e + 1, :]
        hi_e = chunk_hi[e:e + 1, :]
        chunk = jnp.sum(jnp.where(hi_e <= slot_col, 1.0, 0.0), axis=1, keepdims=True)
        first = jnp.max(jnp.where(lo_e <= slot_col, lo_e, 0.0), axis=1, keepdims=True)
        rank = slot_col - first
        pick = jnp.where(lane_sf == chunk, 1.0, 0.0).astype(BF16)
        local = _dot(pick, lc_sc[...].astype(BF16))
        pos = jnp.sum(jnp.where(local <= rank, 1.0, 0.0), axis=1, keepdims=True)
        toks = jnp.where(lane_s == e, chunk * cw + pos, toks)
    tok_ref[...] = toks.astype(I32)


ROUTE_CHUNK = 256


def _route(aff_t, segments, scatter_tile, n_slots):
    bsz, n_exp, L = aff_t.shape
    assert L % ROUTE_CHUNK == 0 and L // ROUTE_CHUNK <= LANES and n_exp <= LANES
    return pl.pallas_call(
        functools.partial(_route_kernel, segments=segments, scatter_tile=scatter_tile, n_slots=n_slots),
        grid=(bsz,),
        in_specs=[pl.BlockSpec((None, n_exp, L), lambda b: (b, 0, 0))],
        out_specs=[pl.BlockSpec((None, n_exp, L), lambda b: (b, 0, 0)),
                   pl.BlockSpec((None, n_exp, LANES), lambda b: (b, 0, 0)),
                   pl.BlockSpec((None, n_slots, LANES), lambda b: (b, 0, 0))],
        out_shape=[jax.ShapeDtypeStruct((bsz, n_exp, L), I32),
                   jax.ShapeDtypeStruct((bsz, n_exp, LANES), I32),
                   jax.ShapeDtypeStruct((bsz, n_slots, LANES), I32)],
        scratch_shapes=[pltpu.VMEM((LANES, ROUTE_CHUNK), F32)],
        compiler_params=_cparams(("arbitrary",), 32),
        name="moe_route",
    )(aff_t)


GATHER_UNROLL = 8


def _gather_kernel(rows_ref, src_ref, o_ref, buf, sem, *, n_slots):
    def issue(s, carry):
        pltpu.make_async_copy(src_ref.at[pl.ds(rows_ref[0, s], 1), :], buf.at[pl.ds(s, 1), :], sem).start()
        return carry
    lax.fori_loop(0, n_slots, issue, 0, unroll=GATHER_UNROLL)

    pltpu.make_async_copy(src_ref.at[pl.ds(0, n_slots), :], buf, sem).wait()
    o_ref[...] = buf[...]


def _gather(rows, b_pack, n_slots):
    n_exp, bsz = rows.shape[:2]
    half = b_pack.shape[1]
    return pl.pallas_call(
        functools.partial(_gather_kernel, n_slots=n_slots),
        grid=(n_exp, bsz),
        in_specs=[
            pl.BlockSpec((None, None, 1, n_slots), lambda e, b: (e, b, 0, 0), memory_space=pltpu.SMEM),
            pl.BlockSpec(memory_space=pl.ANY),
        ],
        out_specs=pl.BlockSpec((None, None, n_slots, half), lambda e, b: (e, b, 0, 0)),
        out_shape=jax.ShapeDtypeStruct((n_exp, bsz, n_slots, half), U32),
        scratch_shapes=[pltpu.VMEM((n_slots, half), U32), pltpu.SemaphoreType.DMA(())],
        compiler_params=_cparams(("arbitrary", "arbitrary"), 32),
        name="moe_gather",
    )(rows, b_pack)


def _expert_up_kernel(x_ref, wg_ref, wu_ref, o_ref, xlo_sc, xhi_sc):
    half = x_ref.shape[1]

    @pl.when(pl.program_id(1) == 0)
    def _unpack():
        w = x_ref[...]
        xlo_sc[...] = pltpu.bitcast(lax.shift_left(w, jnp.uint32(16)), F32).astype(BF16)
        xhi_sc[...] = pltpu.bitcast(w & jnp.uint32(0xFFFF0000), F32).astype(BF16)

    xlo = xlo_sc[...]
    xhi = xhi_sc[...]
    hg = _dot(xlo, wg_ref[0:half, :].astype(BF16)) + _dot(xhi, wg_ref[half:, :].astype(BF16))
    hu = _dot(xlo, wu_ref[0:half, :].astype(BF16)) + _dot(xhi, wu_ref[half:, :].astype(BF16))
    o_ref[...] = (hg * _sigmoid(hg) * hu).astype(BF16)


def _expert_down_kernel(h_ref, wd_ref, o_ref):
    o_ref[...] = _dot(h_ref[...], wd_ref[...].astype(BF16)).astype(BF16)


EXPERT_UP_COLS = 256
EXPERT_DOWN_COLS = 512


def _experts(xs, w_gate, w_up, w_down, layer):
    n_exp, rows, half = xs.shape
    d = 2 * half
    ff = w_gate.shape[3]
    tf = EXPERT_UP_COLS
    hid = pl.pallas_call(
        _expert_up_kernel,
        grid=(n_exp, ff // tf),
        in_specs=[
            pl.BlockSpec((None, rows, half), lambda e, f: (e, 0, 0)),
            pl.BlockSpec((None, None, d, tf), lambda e, f: (layer, e, 0, f)),
            pl.BlockSpec((None, None, d, tf), lambda e, f: (layer, e, 0, f)),
        ],
        out_specs=pl.BlockSpec((None, rows, tf), lambda e, f: (e, 0, f)),
        out_shape=jax.ShapeDtypeStruct((n_exp, rows, ff), BF16),
        scratch_shapes=[pltpu.VMEM((rows, half), BF16), pltpu.VMEM((rows, half), BF16)],
        compiler_params=_cparams(("arbitrary", "arbitrary"), 48),
        name="moe_expert_up",
    )(xs, w_gate, w_up)
    tn = EXPERT_DOWN_COLS
    return pl.pallas_call(
        _expert_down_kernel,
        grid=(n_exp, d // tn),
        in_specs=[
            pl.BlockSpec((None, rows, ff), lambda e, n: (e, 0, 0)),
            pl.BlockSpec((None, None, ff, tn), lambda e, n: (layer, e, 0, n)),
        ],
        out_specs=pl.BlockSpec((None, rows, tn), lambda e, n: (e, 0, n)),
        out_shape=jax.ShapeDtypeStruct((n_exp, rows, d), BF16),
        compiler_params=_cparams(("arbitrary", "arbitrary"), 48),
        name="moe_expert_down",
    )(hid, w_down)


SCATTER_WINDOW = 256
SCATTER_EXPERTS = 2
SCATTER_TILES = 4


def _scatter_kernel(base_ref, keyt_ref, afft_ref, ys_ref, h_ref, gl_ref, gc_ref, gf_ref, o_ref,
                    *, n_slots, ctx_len, tm, n_tiles, final_norm):
    b = pl.program_id(0)
    t = pl.program_id(1)
    eg = pl.program_id(2)
    n_exp = keyt_ref.shape[1]
    win = min(SCATTER_WINDOW, n_slots)
    lane = lax.broadcasted_iota(I32, (tm, n_exp), 1)
    keyt = keyt_ref[...].astype(F32)
    afft = afft_ref[...]
    total = None
    cols = []
    for k in range(SCATTER_EXPERTS):
        e = eg * SCATTER_EXPERTS + k
        pick = lane == e
        kcol = jnp.sum(jnp.where(pick, keyt, 0.0), axis=1, keepdims=True)
        gcol = jnp.sum(jnp.where(pick, afft, 0.0), axis=1, keepdims=True)
        lo = base_ref[(b * (n_tiles + 1) + t) * n_exp + e]
        hi = base_ref[(b * (n_tiles + 1) + t + 1) * n_exp + e]
        start = jnp.minimum((lo // (2 * SUBLANES)) * (2 * SUBLANES), n_slots - win)
        start = pl.multiple_of(start, 2 * SUBLANES)
        slot = (start + lax.broadcasted_iota(I32, (1, win), 1)).astype(F32)
        weights = jnp.where(kcol == slot, gcol, 0.0).astype(BF16)
        part = _dot(weights, ys_ref[k, pl.ds(start, win), :])
        total = part if total is None else total + part
        cols.append((kcol, gcol, start, hi))

    @pl.when(eg == 0)
    def _first():
        o_ref[...] = total

    @pl.when(eg > 0)
    def _rest():
        o_ref[...] += total

    for k, (kcol, gcol, start, hi) in enumerate(cols):
        @pl.when(hi > start + win)
        def _beyond_window():
            slot = lax.broadcasted_iota(I32, (1, n_slots), 1)
            outside = jnp.logical_or(slot < start, slot >= start + win)
            hit = jnp.logical_and(kcol == slot.astype(F32), outside)
            weights = jnp.where(hit, gcol, 0.0).astype(BF16)
            o_ref[...] += _dot(weights, ys_ref[k])

    @pl.when(eg == pl.num_programs(2) - 1)
    def _store():
        row = t * tm + lax.broadcasted_iota(I32, (tm, 1), 0)
        gate = jnp.where(row < ctx_len, gc_ref[...], gl_ref[...])
        hn = h_ref[...] + gate * o_ref[...]
        if final_norm:
            hn = _rms(hn, gf_ref[...])
        o_ref[...] = hn


def _scatter(base, key_t, aff_tt, ys, h, mods3, gfinal, n_slots, ctx_len, final_norm):
    bsz, L, d = h.shape
    n_exp = key_t.shape[2]
    n_tiles = SCATTER_TILES
    tm = L // n_tiles
    assert n_exp % SCATTER_EXPERTS == 0 and (n_slots - min(SCATTER_WINDOW, n_slots)) % (2 * SUBLANES) == 0
    kern = functools.partial(_scatter_kernel, n_slots=n_slots, ctx_len=ctx_len, tm=tm, n_tiles=n_tiles,
                             final_norm=final_norm)
    grid_spec = pltpu.PrefetchScalarGridSpec(
        num_scalar_prefetch=1,
        grid=(bsz, n_tiles, n_exp // SCATTER_EXPERTS),
        in_specs=[
            pl.BlockSpec((None, tm, n_exp), lambda b, t, e, *_: (b, t, 0)),
            pl.BlockSpec((None, tm, n_exp), lambda b, t, e, *_: (b, t, 0)),
            pl.BlockSpec((SCATTER_EXPERTS, None, n_slots, d), lambda b, t, e, *_: (e, b, 0, 0)),
            pl.BlockSpec((None, tm, d), lambda b, t, e, *_: (b, t, 0), pipeline_mode=pl.Buffered(1)),
            pl.BlockSpec((None, 1, d), lambda b, t, e, *_: (b, 0, 5)),
            pl.BlockSpec((None, 1, d), lambda b, t, e, *_: (bsz, 0, 5)),
            pl.BlockSpec((1, d), lambda b, t, e, *_: (0, 0)),
        ],
        out_specs=pl.BlockSpec((None, tm, d), lambda b, t, e, *_: (b, t, 0)),
    )
    return pl.pallas_call(
        kern,
        grid_spec=grid_spec,
        out_shape=jax.ShapeDtypeStruct((bsz, L, d), F32),
        compiler_params=_cparams(("arbitrary", "arbitrary", "arbitrary"), 56),
        name="moe_scatter",
    )(base, key_t, aff_tt, ys, h, mods3, mods3, gfinal)


def _rope_tables(ctx_len, seq):
    pos = np.arange(seq)
    freqs = ROPE_THETA ** (-jnp.arange(0, ROT_AXIS, 2, dtype=F32) / ROT_AXIS)

    def cs(p):
        ang = jnp.asarray(p, F32)[:, None] * freqs[None, :]
        return jnp.cos(ang), jnp.sin(ang)

    cr, sr = cs(pos // GRID_W)
    cc, sc = cs(pos % GRID_W)
    cos_l = jnp.concatenate([cr, cr, cc, cc], axis=1)
    sin_l = jnp.concatenate([-sr, sr, -sc, sc], axis=1)
    cos_t = jnp.concatenate([jnp.ones((ctx_len, HEAD_DIM), F32), cos_l], axis=0)
    sin_t = jnp.concatenate([jnp.zeros((ctx_len, HEAD_DIM), F32), sin_l], axis=0)
    return cos_t, sin_t


def _moe(h, b_act, aff_t, mods3, w_gate, w_up, w_down, layer, gfinal, ctx_len, moe_ctx, final_norm):
    bsz, L, d = h.shape
    seq = L - ctx_len
    n_exp = aff_t.shape[1]
    cap_l = max(1, EC_FACTOR * seq // n_exp)
    cap_c = max(1, EC_FACTOR * ctx_len // n_exp) if moe_ctx else 0
    n_slots = cap_l + cap_c
    segments = ((0, ctx_len, cap_c, 0), (ctx_len, seq, cap_l, cap_c))
    key, base, tok = _route(aff_t, segments, L // SCATTER_TILES, n_slots)
    rows = tok[:, :, :n_exp].transpose(2, 0, 1) + (jnp.arange(bsz, dtype=I32) * L)[None, :, None]
    xs = _gather(rows.reshape(n_exp, bsz, 1, n_slots), b_act.reshape(bsz * L, d // 2), n_slots)
    ys = _experts(xs.reshape(n_exp, bsz * n_slots, d // 2), w_gate, w_up, w_down, layer)
    ys = ys.reshape(n_exp, bsz, n_slots, d)
    base_flat = base[:, :, :SCATTER_TILES + 1].transpose(0, 2, 1).reshape(-1)
    return _scatter(base_flat, key.transpose(0, 2, 1), aff_t.transpose(0, 2, 1), ys, h, mods3, gfinal,
                    n_slots, ctx_len, final_norm)


def kernel(x, c, ctx, c_ctx, mod_w, mod_b, norm1_g, norm2_g, ev_w_in, ev_q_norm, ev_k_norm, ev_conv_w, ev_conv_b, ev_w_out, od_w_in, od_conv_w, od_conv_b, od_wa, od_ba, od_wx, od_bx, od_lam, od_w_out, moe_router, moe_w_gate, moe_w_up, moe_w_down, final_norm_g):
    bsz, seq, d = x.shape
    ctx_len = ctx.shape[1]
    depth = mod_w.shape[0]
    assert bsz < SUBLANES and ctx_len % ROW_TILE == 0 and seq % (2 * KV_CHUNK) == 0
    attn_w = d // 2
    kv_w = attn_w // Q_PER_KV
    conv_w = d // 2
    lru_w = od_w_in.shape[2] // 2

    h = jnp.concatenate([ctx, x], axis=1)
    cvec = jnp.concatenate([c, c_ctx[None, :], jnp.zeros((SUBLANES - bsz - 1, d), F32)], axis=0)
    mods = _mods(cvec, mod_w, mod_b)
    cos_t, sin_t = _rope_tables(ctx_len, seq)
    gfinal = final_norm_g.reshape(1, d)

    for i in range(depth):
        last = i == depth - 1
        mods3 = mods[i].reshape(SUBLANES, 1, 6 * d)
        g1 = norm1_g[i].reshape(1, d)
        g2 = norm2_g[i].reshape(1, d)
        wr = moe_router[i].T
        wr_hi = wr.astype(BF16)
        wr_lo = (wr - wr_hi.astype(F32)).astype(BF16)
        j = i // 2
        if i % 2 == 0:
            q, k, vt, gb, p = _even_in(h, mods3, g1, ev_w_in[j].astype(BF16),
                                      ev_q_norm[j].reshape(1, HEAD_DIM), ev_k_norm[j].reshape(1, HEAD_DIM),
                                      cos_t, sin_t, ctx_len, attn_w, kv_w, conv_w)
            att = _attention(q, k, vt, ctx_len)
            h, b_act, aff_t = _even_out(h, att, gb, p, ev_conv_w[j], ev_conv_b[j].reshape(1, conv_w),
                                        ev_w_out[j].astype(BF16), mods3, g2, wr_hi, wr_lo, ctx_len)
        else:
            xpre, gg = _odd_in(h, mods3, g1, od_w_in[j].astype(BF16), ctx_len, lru_w)
            y2 = _lru(xpre, od_conv_w[j], od_conv_b[j].reshape(1, lru_w),
                      od_wa[j].astype(BF16), od_wx[j].astype(BF16),
                      od_ba[j].reshape(2, 1, lru_w), od_bx[j].reshape(2, 1, lru_w),
                      od_lam[j].reshape(2, 1, lru_w), ctx_len)
            h, b_act, aff_t = _odd_out(h, y2, gg, od_w_out[j].astype(BF16), mods3, g2, wr_hi, wr_lo, ctx_len)
        h = _moe(h, b_act, aff_t, mods3, moe_w_gate, moe_w_up, moe_w_down, i, gfinal,
                 ctx_len, moe_ctx=not last, final_norm=last)
    return h[:, ctx_len:, :]
```

```python
import functools

import jax
import jax.numpy as jnp
import numpy as np
from jax import lax
from jax.experimental import pallas as pl
from jax.experimental.pallas import tpu as pltpu

F32 = jnp.float32
BF16 = jnp.bfloat16
I32 = jnp.int32
U32 = jnp.uint32

EPS = 1e-6
HEAD_DIM = 128
ROT_AXIS = HEAD_DIM // 2
ROPE_THETA = 10000.0
GRID_W = 64
Q_PER_KV = 4
LRU_BLOCK = 256
LRU_C = 8.0
N_EXPERTS = 16
EC_FACTOR = 2

LANES = 128
SUBLANES = 8
ROW_TILE = 256
KV_CHUNK = 512
MIB = 1024 * 1024


def _cparams(sem, vmem_mib):
    return pltpu.CompilerParams(dimension_semantics=sem, vmem_limit_bytes=int(vmem_mib * MIB))


def _sigmoid(x):
    return 1.0 / (1.0 + jnp.exp(-x))


def _rms(x, g):
    return x * lax.rsqrt(jnp.mean(x * x, axis=-1, keepdims=True) + EPS) * g


def _dot(a, b):
    return jnp.dot(a, b, preferred_element_type=F32)


def _dot_nt(a, b):
    return lax.dot_general(a, b, (((1,), (1,)), ((), ())), preferred_element_type=F32)


def _resident(shape):
    nd = len(shape)
    return pl.BlockSpec(shape, lambda *_: (0,) * nd)


def _mods_kernel(c_ref, w_ref, b_ref, o_ref):
    c = c_ref[...]
    s = (c * _sigmoid(c)).astype(BF16)
    o_ref[...] = _dot(s, w_ref[...].astype(BF16)) + b_ref[...]


def _mods(cvec, mod_w, mod_b):
    depth, d, n6 = mod_w.shape
    tn = 1536
    return pl.pallas_call(
        _mods_kernel,
        grid=(depth, n6 // tn),
        in_specs=[
            pl.BlockSpec((SUBLANES, d), lambda l, j: (0, 0)),
            pl.BlockSpec((None, d, tn), lambda l, j: (l, 0, j)),
            pl.BlockSpec((None, 1, tn), lambda l, j: (l, 0, j)),
        ],
        out_specs=pl.BlockSpec((None, SUBLANES, tn), lambda l, j: (l, 0, j)),
        out_shape=jax.ShapeDtypeStruct((depth, SUBLANES, n6), F32),
        compiler_params=_cparams(("arbitrary", "arbitrary"), 40),
        name="adaln_mods",
    )(cvec, mod_w, mod_b.reshape(depth, 1, n6))


def _mod_spec(bsz, n_ctx_tiles, d, k):
    return pl.BlockSpec((None, 1, d), lambda b, i: (jnp.where(i < n_ctx_tiles, bsz, b), 0, k))


def _stream_parts(h):
    return tuple(h) if isinstance(h, (tuple, list)) else (h,)


def _stream_specs(parts, tm, n_ctx_tiles):
    d = parts[0].shape[2]
    if len(parts) == 1:
        return [pl.BlockSpec((None, tm, d), lambda b, i: (b, i, 0))]
    return [pl.BlockSpec((None, tm, d), lambda b, i: (b, jnp.minimum(i, n_ctx_tiles - 1), 0)),
            pl.BlockSpec((None, tm, d), lambda b, i: (b, jnp.maximum(i - n_ctx_tiles, 0), 0))]


def _stream_tile(refs, n_ctx_tiles):
    if len(refs) == 1:
        return refs[0][...]
    return jnp.where(pl.program_id(1) < n_ctx_tiles, refs[0][...], refs[1][...])


def _even_in_kernel(*refs, n_parts, n_ctx_tiles, attn_w, kv_w, conv_w):
    (g1_ref, sh_ref, sc_ref, w_ref, qg_ref, kg_ref, cos_ref, sin_ref,
     q_ref, k_ref, v_ref, gb_ref, p_ref) = refs[n_parts:]
    x = _stream_tile(refs[:n_parts], n_ctx_tiles)
    a = (_rms(x, g1_ref[...]) * (1.0 + sc_ref[...]) + sh_ref[...]).astype(BF16)
    cos = cos_ref[...]
    sin = sin_ref[...]
    tm = x.shape[0]
    lane = lax.broadcasted_iota(I32, (tm, HEAD_DIM), 1)
    low_half = (lane % ROT_AXIS) < (ROT_AXIS // 2)

    def norm_rope(t, g, scale):
        t = t * lax.rsqrt(jnp.mean(t * t, axis=-1, keepdims=True) + EPS) * g
        swapped = jnp.where(low_half, pltpu.roll(t, HEAD_DIM - ROT_AXIS // 2, 1),
                            pltpu.roll(t, ROT_AXIS // 2, 1))
        return (t * cos + swapped * sin) * scale

    cw = 4 * HEAD_DIM
    qg = qg_ref[...]
    kg = kg_ref[...]
    q_scale = HEAD_DIM ** -0.5 * np.log2(np.e)
    for c0 in range(0, attn_w, cw):
        y = _dot(a, w_ref[:, c0:c0 + cw])
        for j in range(cw // HEAD_DIM):
            t = norm_rope(y[:, j * HEAD_DIM:(j + 1) * HEAD_DIM], qg, q_scale)
            q_ref[:, c0 + j * HEAD_DIM:c0 + (j + 1) * HEAD_DIM] = t.astype(BF16)
    y = _dot(a, w_ref[:, attn_w:attn_w + 2 * kv_w])
    for j in range(kv_w // HEAD_DIM):
        t = norm_rope(y[:, j * HEAD_DIM:(j + 1) * HEAD_DIM], kg, 1.0)
        k_ref[:, j * HEAD_DIM:(j + 1) * HEAD_DIM] = t.astype(BF16)
    v_ref[...] = y[:, kv_w:2 * kv_w].T.astype(BF16)
    off_b = attn_w + 2 * kv_w
    off_c = off_b + conv_w
    off_u = off_c + conv_w
    for c0 in range(0, conv_w, cw):
        gb_ref[:, c0:c0 + cw] = _dot(a, w_ref[:, off_b + c0:off_b + c0 + cw]).astype(BF16)
        cc = _dot(a, w_ref[:, off_c + c0:off_c + c0 + cw])
        uu = _dot(a, w_ref[:, off_u + c0:off_u + c0 + cw])
        p_ref[:, c0:c0 + cw] = (cc * uu).astype(BF16)


def _even_in(h, mods3, g1, w_in, qg, kg, cos_t, sin_t, ctx_len, attn_w, kv_w, conv_w):
    parts = _stream_parts(h)
    bsz, _, d = parts[0].shape
    L = sum(part.shape[1] for part in parts)
    tm = ROW_TILE
    nct = ctx_len // tm
    tok = lambda width: pl.BlockSpec((None, tm, width), lambda b, i: (b, i, 0))
    kern = functools.partial(_even_in_kernel, n_parts=len(parts), n_ctx_tiles=nct,
                             attn_w=attn_w, kv_w=kv_w, conv_w=conv_w)
    return pl.pallas_call(
        kern,
        grid=(bsz, L // tm),
        in_specs=_stream_specs(parts, tm, nct) + [
            _resident((1, d)),
            _mod_spec(bsz, nct, d, 0),
            _mod_spec(bsz, nct, d, 1),
            _resident(w_in.shape),
            _resident((1, HEAD_DIM)),
            _resident((1, HEAD_DIM)),
            pl.BlockSpec((tm, HEAD_DIM), lambda b, i: (i, 0)),
            pl.BlockSpec((tm, HEAD_DIM), lambda b, i: (i, 0)),
        ],
        out_specs=[tok(attn_w), tok(kv_w), pl.BlockSpec((None, kv_w, tm), lambda b, i: (b, 0, i)),
                   tok(conv_w), tok(conv_w)],
        out_shape=[
            jax.ShapeDtypeStruct((bsz, L, attn_w), BF16),
            jax.ShapeDtypeStruct((bsz, L, kv_w), BF16),
            jax.ShapeDtypeStruct((bsz, kv_w, L), BF16),
            jax.ShapeDtypeStruct((bsz, L, conv_w), BF16),
            jax.ShapeDtypeStruct((bsz, L, conv_w), BF16),
        ],
        compiler_params=_cparams(("arbitrary", "arbitrary"), 56),
        name="even_in_proj",
    )(*parts, g1, mods3, mods3, w_in, qg, kg, cos_t, sin_t)


def _attn_kernel(q_ref, k_ref, vt_ref, o_ref, m_sc, l_sc, acc_sc, sa_sc, sb_sc, *, ctx_len, n_lat_chunks, tq):
    qi = pl.program_id(2)
    q = q_ref[...]
    qs = jnp.concatenate([q[:, j * HEAD_DIM:(j + 1) * HEAD_DIM] for j in range(Q_PER_KV)], axis=0)

    def lat_start(c):
        return pl.multiple_of(ctx_len + c * KV_CHUNK, KV_CHUNK // 2)

    def scores(c):
        return _dot_nt(k_ref[pl.ds(lat_start(c), KV_CHUNK), :], qs)

    def absorb(s_ref, c):
        sc = s_ref[...]
        m_old = m_sc[...]
        m_new = jnp.maximum(m_old, jnp.max(sc, axis=0, keepdims=True))
        alpha = jnp.exp2(m_old - m_new)
        pc = jnp.exp2(sc - m_new)
        l_sc[...] = alpha * l_sc[...] + jnp.sum(pc, axis=0, keepdims=True)
        acc_sc[...] = alpha * acc_sc[...] + _dot(vt_ref[:, pl.ds(lat_start(c), KV_CHUNK)], pc.astype(BF16))
        m_sc[...] = m_new

    s = _dot_nt(k_ref[0:ctx_len, :], qs)
    m0 = jnp.max(s, axis=0, keepdims=True)
    p = jnp.exp2(s - m0)
    m_sc[...] = m0
    l_sc[...] = jnp.sum(p, axis=0, keepdims=True)
    acc_sc[...] = _dot(vt_ref[:, 0:ctx_len], p.astype(BF16))

    @pl.when(qi >= ctx_len // tq)
    def _latent_keys():
        sa_sc[...] = scores(0)

        def body(i, carry):
            c = 2 * i
            sb_sc[...] = scores(c + 1)
            absorb(sa_sc, c)
            sa_sc[...] = scores(jnp.minimum(c + 2, n_lat_chunks - 1))
            absorb(sb_sc, c + 1)
            return carry
        lax.fori_loop(0, n_lat_chunks // 2, body, 0)

    o = acc_sc[...] * (1.0 / l_sc[...])
    for j in range(Q_PER_KV):
        o_ref[:, j * HEAD_DIM:(j + 1) * HEAD_DIM] = o[:, j * tq:(j + 1) * tq].T.astype(BF16)


def _attention(q, k, vt, ctx_len):
    bsz, L, attn_w = q.shape
    n_kv = k.shape[2] // HEAD_DIM
    tq = ROW_TILE
    gw = Q_PER_KV * HEAD_DIM
    seq = L - ctx_len
    kern = functools.partial(_attn_kernel, ctx_len=ctx_len, n_lat_chunks=seq // KV_CHUNK, tq=tq)
    return pl.pallas_call(
        kern,
        grid=(bsz, n_kv, L // tq),
        in_specs=[
            pl.BlockSpec((None, tq, gw), lambda b, g, i: (b, i, g)),
            pl.BlockSpec((None, L, HEAD_DIM), lambda b, g, i: (b, 0, g)),
            pl.BlockSpec((None, HEAD_DIM, L), lambda b, g, i: (b, g, 0)),
        ],
        out_specs=pl.BlockSpec((None, tq, gw), lambda b, g, i: (b, i, g)),
        out_shape=jax.ShapeDtypeStruct((bsz, L, attn_w), BF16),
        scratch_shapes=[
            pltpu.VMEM((1, Q_PER_KV * tq), F32),
            pltpu.VMEM((1, Q_PER_KV * tq), F32),
            pltpu.VMEM((HEAD_DIM, Q_PER_KV * tq), F32),
            pltpu.VMEM((KV_CHUNK, Q_PER_KV * tq), F32),
            pltpu.VMEM((KV_CHUNK, Q_PER_KV * tq), F32),
        ],
        compiler_params=_cparams(("arbitrary", "arbitrary", "arbitrary"), 40),
        name="attention",
    )(q, k, vt)


def _gelu_tanh(x):
    return 0.5 * x * (1.0 + jnp.tanh(np.sqrt(2.0 / np.pi) * (x + 0.044715 * (x * x * x))))


def _odd_in_kernel(h_ref, g1_ref, sh_ref, sc_ref, w_ref, cw_ref, cb_ref, u_ref, gg_ref, edge_ref, *, lru_w):
    x = h_ref[...]
    a = (_rms(x, g1_ref[...]) * (1.0 + sc_ref[...]) + sh_ref[...]).astype(BF16)
    cw = 512
    tm = x.shape[0]
    sub8 = lax.broadcasted_iota(I32, (SUBLANES, 1), 0)
    for c0 in range(0, lru_w, cw):
        cs = slice(c0, c0 + cw)
        y = _dot(a, w_ref[:, cs])
        w0, w1, w2, w3 = (cw_ref[k:k + 1, cs] for k in range(4))
        u = (pltpu.roll(y, 1, 0) * w0 + y * w1 + pltpu.roll(y, tm - 1, 0) * w2 + pltpu.roll(y, tm - 2, 0) * w3
             + cb_ref[:, cs])
        wrap_first = jnp.where(sub8 == 0, y[tm - 1:tm, :] * w0, 0.0)
        wrap_last = jnp.where(sub8 == SUBLANES - 1, y[0:1, :] * w2 + y[1:2, :] * w3,
                              jnp.where(sub8 == SUBLANES - 2, y[0:1, :] * w3, 0.0))
        u_ref[0:SUBLANES, cs] = u[0:SUBLANES] - wrap_first
        u_ref[SUBLANES:tm - SUBLANES, cs] = u[SUBLANES:tm - SUBLANES]
        u_ref[tm - SUBLANES:tm, cs] = u[tm - SUBLANES:] - wrap_last
        edge_ref[:, cs] = jnp.where(sub8 == 0, y[0:1, :], jnp.where(sub8 == 1, y[1:2, :],
                                    jnp.where(sub8 == 2, y[tm - 1:tm, :], 0.0)))
        gg_ref[:, cs] = _gelu_tanh(_dot(a, w_ref[:, lru_w + c0:lru_w + c0 + cw])).astype(BF16)


def _odd_in(h, mods3, g1, w_in, conv_w, conv_b, ctx_len, lru_w):
    bsz, L, d = h.shape
    tm = ROW_TILE
    nct = ctx_len // tm
    tok = lambda width: pl.BlockSpec((None, tm, width), lambda b, i: (b, i, 0))
    return pl.pallas_call(
        functools.partial(_odd_in_kernel, lru_w=lru_w),
        grid=(bsz, L // tm),
        in_specs=[tok(d), _resident((1, d)), _mod_spec(bsz, nct, d, 0), _mod_spec(bsz, nct, d, 1),
                  _resident(w_in.shape), _resident(conv_w.shape), _resident((1, lru_w))],
        out_specs=[tok(lru_w), tok(lru_w),
                   pl.BlockSpec((None, None, SUBLANES, lru_w), lambda b, i: (b, i, 0, 0))],
        out_shape=[jax.ShapeDtypeStruct((bsz, L, lru_w), F32),
                   jax.ShapeDtypeStruct((bsz, L, lru_w), BF16),
                   jax.ShapeDtypeStruct((bsz, L // tm, SUBLANES, lru_w), F32)],
        compiler_params=_cparams(("arbitrary", "arbitrary"), 56),
        name="odd_in_proj",
    )(h, g1, mods3, mods3, w_in, conv_w, conv_b)


def _lru_tile(d, j, n_tiles, n_ctx_tiles):
    rev = jnp.where(j < n_ctx_tiles, n_ctx_tiles - 1 - j, n_tiles - 1 - (j - n_ctx_tiles))
    return jnp.where(d == 0, j, rev)


def _lru_kernel(u_ref, ep_ref, en_ref, cw_ref, wa_ref, wx_ref, ba_ref, bx_ref, lam_ref,
                y_ref, a_sc, b_sc, y_sc, h_sc, *, n_tiles, n_ctx_tiles, tt):
    d = pl.program_id(1)
    j = pl.program_id(2)
    tile = _lru_tile(d, j, n_tiles, n_ctx_tiles)
    width = u_ref.shape[1]
    seg_first = jnp.logical_or(tile == 0, tile == n_ctx_tiles)
    seg_last = jnp.logical_or(tile == n_ctx_tiles - 1, tile == n_tiles - 1)
    prev = jnp.where(seg_first, 0.0, ep_ref[2:3, :])
    nxt1 = jnp.where(seg_last, 0.0, en_ref[0:1, :])
    nxt2 = jnp.where(seg_last, 0.0, en_ref[1:2, :])
    w0, w2, w3 = cw_ref[0:1, :], cw_ref[2:3, :], cw_ref[3:4, :]
    sub8 = lax.broadcasted_iota(I32, (SUBLANES, 1), 0)
    add_first = jnp.where(sub8 == 0, prev * w0, 0.0)
    add_last = jnp.where(sub8 == SUBLANES - 1, nxt1 * w2 + nxt2 * w3,
                         jnp.where(sub8 == SUBLANES - 2, nxt1 * w3, 0.0))
    u = jnp.concatenate([u_ref[0:SUBLANES, :] + add_first, u_ref[SUBLANES:tt - SUBLANES, :],
                         u_ref[tt - SUBLANES:tt, :] + add_last], axis=0)
    z = -lam_ref[...]
    la = (-LRU_C * np.log2(np.e)) * (jnp.maximum(z, 0.0) + jnp.log(1.0 + jnp.exp(-jnp.abs(z))))
    for blk in range(width // LRU_BLOCK):
        cs = slice(blk * LRU_BLOCK, (blk + 1) * LRU_BLOCK)
        ub = u[:, cs]
        ub16 = ub.astype(BF16)
        r = _sigmoid(_dot(ub16, wa_ref[blk]) + ba_ref[:, cs])
        ig = _sigmoid(_dot(ub16, wx_ref[blk]) + bx_ref[:, cs])
        a = jnp.exp2(r * la[:, cs])
        a_sc[:, cs] = a
        om = 1.0 - a * a
        b_sc[:, cs] = jnp.where(om > 0.0, om * lax.rsqrt(om), 0.0) * ig * ub

    @pl.when(j == 0)
    def _reset():
        h_sc[...] = jnp.zeros_like(h_sc)

    sub = lax.broadcasted_iota(I32, (SUBLANES, width), 0)
    n_groups = tt // SUBLANES

    def group_scan(base, forward):
        av = a_sc[pl.ds(base, SUBLANES), :]
        bv = b_sc[pl.ds(base, SUBLANES), :]
        for s in (1, 2, 4):
            shift = s if forward else SUBLANES - s
            keep = (sub >= s) if forward else (sub < SUBLANES - s)
            a_sh = pltpu.roll(av, shift, 0)
            b_sh = pltpu.roll(bv, shift, 0)
            bv = jnp.where(keep, av * b_sh + bv, bv)
            av = jnp.where(keep, av * a_sh, av)
        out = av * h_sc[...] + bv
        y_sc[pl.ds(base, SUBLANES), :] = out
        last = out[SUBLANES - 1:SUBLANES, :] if forward else out[0:1, :]
        h_sc[...] = jnp.broadcast_to(last, (SUBLANES, width))

    @pl.when(d == 0)
    def _forward():
        def body(g, carry):
            group_scan(pl.multiple_of(g * SUBLANES, SUBLANES), True)
            return carry
        lax.fori_loop(0, n_groups, body, 0)

    @pl.when(d == 1)
    def _backward():
        def body(g, carry):
            group_scan(pl.multiple_of((n_groups - 1 - g) * SUBLANES, SUBLANES), False)
            return carry
        lax.fori_loop(0, n_groups, body, 0)

    y_ref[...] = y_sc[...].astype(BF16)


def _lru(u, edge, conv_w, wa, wx, ba, bx, lam, ctx_len):
    bsz, L, width = u.shape
    tt = ROW_TILE
    nt = L // tt
    nct = ctx_len // tt
    nb = width // LRU_BLOCK
    tile = lambda d, j: _lru_tile(d, j, nt, nct)
    kern = functools.partial(_lru_kernel, n_tiles=nt, n_ctx_tiles=nct, tt=tt)
    dir_vec = pl.BlockSpec((None, 1, width), lambda b, d, j: (d, 0, 0))
    dir_mat = pl.BlockSpec((None, nb, LRU_BLOCK, LRU_BLOCK), lambda b, d, j: (d, 0, 0, 0))
    return pl.pallas_call(
        kern,
        grid=(bsz, 2, nt),
        in_specs=[
            pl.BlockSpec((None, tt, width), lambda b, d, j: (b, tile(d, j), 0)),
            pl.BlockSpec((None, None, SUBLANES, width),
                         lambda b, d, j: (b, jnp.maximum(tile(d, j) - 1, 0), 0, 0)),
            pl.BlockSpec((None, None, SUBLANES, width),
                         lambda b, d, j: (b, jnp.minimum(tile(d, j) + 1, nt - 1), 0, 0)),
            pl.BlockSpec(conv_w.shape, lambda b, d, j: (0, 0)),
            dir_mat, dir_mat, dir_vec, dir_vec, dir_vec,
        ],
        out_specs=pl.BlockSpec((None, None, tt, width), lambda b, d, j: (d, b, tile(d, j), 0)),
        out_shape=jax.ShapeDtypeStruct((2, bsz, L, width), BF16),
        scratch_shapes=[pltpu.VMEM((tt, width), F32), pltpu.VMEM((tt, width), F32),
                        pltpu.VMEM((tt, width), F32), pltpu.VMEM((SUBLANES, width), F32)],
        compiler_params=_cparams(("arbitrary", "arbitrary", "arbitrary"), 48),
        name="rglru_scan",
    )(u, edge, edge, conv_w, wa, wx, ba, bx, lam)


def _out_tail(h, y, gate_ref, g2_ref, sh2_ref, sc2_ref, wrh_ref, wrl_ref, ho_ref, bo_ref, aff_ref):
    hn = h + gate_ref[...] * y
    ho_ref[...] = hn
    bn = _rms(hn, g2_ref[...]) * (1.0 + sc2_ref[...]) + sh2_ref[...]
    b_hi = bn.astype(BF16)
    half = bn.shape[1] // 2
    lo = lax.shift_right_logical(pltpu.bitcast(b_hi[:, :half].astype(F32), U32), jnp.uint32(16))
    hi = pltpu.bitcast(b_hi[:, half:].astype(F32), U32) & jnp.uint32(0xFFFF0000)
    bo_ref[...] = lo | hi
    b_lo = (bn - b_hi.astype(F32)).astype(BF16)
    wrh = wrh_ref[...]
    logits = _dot_nt(wrh, b_hi) + _dot_nt(wrh, b_lo) + _dot_nt(wrl_ref[...], b_hi)
    ex = jnp.exp(logits - jnp.max(logits, axis=0, keepdims=True))
    aff_ref[...] = ex / jnp.sum(ex, axis=0, keepdims=True)


def _even_out_kernel(*refs, n_parts, n_tiles, n_ctx_tiles, attn_w):
    (att_ref, gb_ref, p_ref, pp_ref, pn_ref, cw_ref, cb_ref, w_ref,
     gate_ref, g2_ref, sh2_ref, sc2_ref, wrh_ref, wrl_ref, ho_ref, bo_ref, aff_ref) = refs[n_parts:]
    h = _stream_tile(refs[:n_parts], n_ctx_tiles)
    i = pl.program_id(1)
    tm = p_ref.shape[0]
    p = p_ref[...].astype(F32)
    seg_first = jnp.logical_or(i == 0, i == n_ctx_tiles)
    seg_last = jnp.logical_or(i == n_ctx_tiles - 1, i == n_tiles - 1)
    prev = jnp.where(seg_first, 0.0, pp_ref[...].astype(F32)[2 * SUBLANES - 1:2 * SUBLANES, :])
    nxt = jnp.where(seg_last, 0.0, pn_ref[...].astype(F32)[0:1, :])
    row = lax.broadcasted_iota(I32, (tm, 1), 0)
    p_m1 = jnp.where(row == 0, prev, pltpu.roll(p, 1, 0))
    p_p1 = jnp.where(row == tm - 1, nxt, pltpu.roll(p, tm - 1, 0))
    conv = p_m1 * cw_ref[0:1, :] + p * cw_ref[1:2, :] + p_p1 * cw_ref[2:3, :] + cb_ref[...]
    act = (gb_ref[...].astype(F32) * conv).astype(BF16)
    y = _dot(att_ref[...], w_ref[0:attn_w, :]) + _dot(act, w_ref[attn_w:, :])
    _out_tail(h, y, gate_ref, g2_ref, sh2_ref, sc2_ref, wrh_ref, wrl_ref, ho_ref, bo_ref, aff_ref)


def _odd_out_kernel(h_ref, yf_ref, yr_ref, gg_ref, w_ref,
                    gate_ref, g2_ref, sh2_ref, sc2_ref, wrh_ref, wrl_ref,
                    ho_ref, bo_ref, aff_ref):
    ysum = (yf_ref[...].astype(F32) + yr_ref[...].astype(F32)).astype(BF16)
    act = (ysum.astype(F32) * gg_ref[...].astype(F32)).astype(BF16)
    y = _dot(act, w_ref[...])
    _out_tail(h_ref[...], y, gate_ref, g2_ref, sh2_ref, sc2_ref, wrh_ref, wrl_ref, ho_ref, bo_ref, aff_ref)


def _out_common(bsz, L, d, ctx_len, n_exp):
    tm = ROW_TILE
    nct = ctx_len // tm
    tail_specs = [
        _mod_spec(bsz, nct, d, 2), _resident((1, d)), _mod_spec(bsz, nct, d, 3), _mod_spec(bsz, nct, d, 4),
        _resident((n_exp, d)), _resident((n_exp, d)),
    ]
    out_specs = [
        pl.BlockSpec((None, tm, d), lambda b, i: (b, i, 0)),
        pl.BlockSpec((None, tm, d // 2), lambda b, i: (b, i, 0)),
        pl.BlockSpec((None, n_exp, tm), lambda b, i: (b, 0, i)),
    ]
    out_shape = [
        jax.ShapeDtypeStruct((bsz, L, d), F32),
        jax.ShapeDtypeStruct((bsz, L, d // 2), U32),
        jax.ShapeDtypeStruct((bsz, n_exp, L), F32),
    ]
    return tm, nct, tail_specs, out_specs, out_shape


def _even_out(h, att, gb, p, conv_w, conv_b, w_out, mods3, g2, wr_hi, wr_lo, ctx_len):
    parts = _stream_parts(h)
    bsz, _, d = parts[0].shape
    L = sum(part.shape[1] for part in parts)
    n_exp = wr_hi.shape[0]
    tm, nct, tail_specs, out_specs, out_shape = _out_common(bsz, L, d, ctx_len, n_exp)
    attn_w = att.shape[2]
    conv_wd = p.shape[2]
    halo = 2 * SUBLANES
    hb = tm // halo
    tok = lambda width: pl.BlockSpec((None, tm, width), lambda b, i: (b, i, 0))
    kern = functools.partial(_even_out_kernel, n_parts=len(parts), n_tiles=L // tm, n_ctx_tiles=nct,
                             attn_w=attn_w)
    return pl.pallas_call(
        kern,
        grid=(bsz, L // tm),
        in_specs=_stream_specs(parts, tm, nct) + [
            tok(attn_w), tok(conv_wd), tok(conv_wd),
            pl.BlockSpec((None, halo, conv_wd), lambda b, i: (b, jnp.maximum(i * hb - 1, 0), 0)),
            pl.BlockSpec((None, halo, conv_wd), lambda b, i: (b, jnp.minimum((i + 1) * hb, L // halo - 1), 0)),
            _resident(conv_w.shape), _resident((1, conv_wd)), _resident(w_out.shape),
        ] + tail_specs,
        out_specs=out_specs,
        out_shape=out_shape,
        compiler_params=_cparams(("arbitrary", "arbitrary"), 48),
        name="even_out_proj",
    )(*parts, att, gb, p, p, p, conv_w, conv_b, w_out, mods3, g2, mods3, mods3, wr_hi, wr_lo)


def _odd_out(h, y2, gg, w_out, mods3, g2, wr_hi, wr_lo, ctx_len, latent_only):
    bsz, L, d = h.shape
    n_exp = wr_hi.shape[0]
    width = gg.shape[2]
    skip = ctx_len // ROW_TILE if latent_only else 0
    L_out = L - skip * ROW_TILE
    tm, nct, tail_specs, out_specs, out_shape = _out_common(bsz, L_out, d, 0 if latent_only else ctx_len, n_exp)
    tok = lambda wd: pl.BlockSpec((None, tm, wd), lambda b, i: (b, i + skip, 0))
    return pl.pallas_call(
        _odd_out_kernel,
        grid=(bsz, L_out // tm),
        in_specs=[
            tok(d),
            pl.BlockSpec((None, None, tm, width), lambda b, i: (0, b, i + skip, 0)),
            pl.BlockSpec((None, None, tm, width), lambda b, i: (1, b, i + skip, 0)),
            tok(width), _resident(w_out.shape),
        ] + tail_specs,
        out_specs=out_specs,
        out_shape=out_shape,
        compiler_params=_cparams(("arbitrary", "arbitrary"), 48),
        name="odd_out_proj",
    )(h, y2, y2, gg, w_out, mods3, g2, mods3, mods3, wr_hi, wr_lo)


def _route_kernel(aff_ref, key_ref, base_ref, tok_ref, lc_sc, *, segments, scatter_tile, n_slots):
    n_exp, L = aff_ref.shape
    for (start, n, cap, slot_off) in segments:
        if n == 0:
            continue
        if cap == 0:
            key_ref[:, start:start + n] = jnp.full((n_exp, n), -1, I32)
            continue
        aff = aff_ref[:, start:start + n]
        bits = pltpu.bitcast(aff, I32)
        capf = jnp.float32(cap)

        def count(mask):
            return jnp.sum(jnp.where(mask, 1.0, 0.0), axis=1, keepdims=True)

        thr = jnp.zeros((n_exp, 1), I32)
        for bit in range(30, -1, -1):
            cand = thr | jnp.int32(1 << bit)
            thr = jnp.where(count(bits >= cand) >= capf, cand, thr)
        gt = bits > thr
        eq = bits == thr
        need = capf - count(gt)

        cwid = 256 if n % 256 == 0 else LANES
        ri = lax.broadcasted_iota(I32, (cwid, cwid), 0)
        ci = lax.broadcasted_iota(I32, (cwid, cwid), 1)
        upper = jnp.where(ri <= ci, 1.0, 0.0).astype(BF16)

        def cumsum_excl(xf):
            outs = []
            carry = jnp.zeros((n_exp, 1), F32)
            for c in range(n // cwid):
                xc = xf[:, c * cwid:(c + 1) * cwid]
                inc = _dot(xc.astype(BF16), upper)
                outs.append(inc - xc + carry)
                carry = carry + inc[:, cwid - 1:cwid]
            return jnp.concatenate(outs, axis=1)

        eqf = jnp.where(eq, 1.0, 0.0)
        sel = jnp.logical_or(gt, jnp.logical_and(eq, cumsum_excl(eqf) < need))
        self_f = jnp.where(sel, 1.0, 0.0)
        slot = cumsum_excl(self_f) + jnp.float32(slot_off)
        key_ref[:, start:start + n] = jnp.where(sel, slot, -1.0).astype(I32)

    taken = jnp.where(key_ref[...] >= 0, 1.0, 0.0)
    tok = lax.broadcasted_iota(I32, (n_exp, L), 1)
    lane = lax.broadcasted_iota(I32, (n_exp, LANES), 1)
    base = jnp.zeros((n_exp, LANES), F32)
    for k in range(1, L // scatter_tile + 1):
        cnt = jnp.sum(jnp.where(tok < k * scatter_tile, taken, 0.0), axis=1, keepdims=True)
        base = jnp.where(lane == k, cnt, base)
    base_ref[...] = base.astype(I32)

    cw = ROUTE_CHUNK
    nch = L // cw
    ri = lax.broadcasted_iota(I32, (cw, cw), 0)
    ci = lax.broadcasted_iota(I32, (cw, cw), 1)
    upper = jnp.where(ri <= ci, 1.0, 0.0).astype(BF16)
    incs = []
    big = jnp.float32(2 ** 30)
    chunk_lo = jnp.full((n_exp, LANES), big, F32)
    chunk_hi = jnp.full((n_exp, LANES), big, F32)
    run = jnp.zeros((n_exp, 1), F32)
    for j in range(nch):
        inc = _dot(taken[:, j * cw:(j + 1) * cw].astype(BF16), upper)
        incs.append(inc)
        chunk_lo = jnp.where(lane == j, run, chunk_lo)
        run = run + inc[:, cw - 1:cw]
        chunk_hi = jnp.where(lane == j, run, chunk_hi)
    lc_sc[...] = jnp.zeros_like(lc_sc)
    slot_col = lax.broadcasted_iota(I32, (n_slots, 1), 0).astype(F32)
    lane_s = lax.broadcasted_iota(I32, (n_slots, LANES), 1)
    lane_sf = lane_s.astype(F32)
    toks = jnp.zeros((n_slots, LANES), F32)
    for e in range(n_exp):
        for j in range(nch):
            lc_sc[j:j + 1, :] = incs[j][e:e + 1, :]
        lo_e = chunk_lo[e:e + 1, :]
        hi_e = chunk_hi[e:e + 1, :]
        chunk = jnp.sum(jnp.where(hi_e <= slot_col, 1.0, 0.0), axis=1, keepdims=True)
        first = jnp.max(jnp.where(lo_e <= slot_col, lo_e, 0.0), axis=1, keepdims=True)
        rank = slot_col - first
        pick = jnp.where(lane_sf == chunk, 1.0, 0.0).astype(BF16)
        local = _dot(pick, lc_sc[...].astype(BF16))
        pos = jnp.sum(jnp.where(local <= rank, 1.0, 0.0), axis=1, keepdims=True)
        toks = jnp.where(lane_s == e, chunk * cw + pos, toks)
    tok_ref[...] = toks.astype(I32)


ROUTE_CHUNK = 256


def _route(aff_t, segments, scatter_tile, n_slots):
    bsz, n_exp, L = aff_t.shape
    assert L % ROUTE_CHUNK == 0 and L // ROUTE_CHUNK <= LANES and n_exp <= LANES
    return pl.pallas_call(
        functools.partial(_route_kernel, segments=segments, scatter_tile=scatter_tile, n_slots=n_slots),
        grid=(bsz,),
        in_specs=[pl.BlockSpec((None, n_exp, L), lambda b: (b, 0, 0))],
        out_specs=[pl.BlockSpec((None, n_exp, L), lambda b: (b, 0, 0)),
                   pl.BlockSpec((None, n_exp, LANES), lambda b: (b, 0, 0)),
                   pl.BlockSpec((None, n_slots, LANES), lambda b: (b, 0, 0))],
        out_shape=[jax.ShapeDtypeStruct((bsz, n_exp, L), I32),
                   jax.ShapeDtypeStruct((bsz, n_exp, LANES), I32),
                   jax.ShapeDtypeStruct((bsz, n_slots, LANES), I32)],
        scratch_shapes=[pltpu.VMEM((LANES, ROUTE_CHUNK), F32)],
        compiler_params=_cparams(("arbitrary",), 32),
        name="moe_route",
    )(aff_t)


GATHER_UNROLL = 8


def _gather_kernel(rows_ref, src_ref, o_ref, buf, sem, *, n_slots):
    def issue(s, carry):
        pltpu.make_async_copy(src_ref.at[pl.ds(rows_ref[0, s], 1), :], buf.at[pl.ds(s, 1), :], sem).start()
        return carry
    lax.fori_loop(0, n_slots, issue, 0, unroll=GATHER_UNROLL)

    pltpu.make_async_copy(src_ref.at[pl.ds(0, n_slots), :], buf, sem).wait()
    o_ref[...] = buf[...]


def _gather(rows, b_pack, n_slots):
    n_exp, bsz = rows.shape[:2]
    half = b_pack.shape[1]
    return pl.pallas_call(
        functools.partial(_gather_kernel, n_slots=n_slots),
        grid=(n_exp, bsz),
        in_specs=[
            pl.BlockSpec((None, None, 1, n_slots), lambda e, b: (e, b, 0, 0), memory_space=pltpu.SMEM),
            pl.BlockSpec(memory_space=pl.ANY),
        ],
        out_specs=pl.BlockSpec((None, None, n_slots, half), lambda e, b: (e, b, 0, 0)),
        out_shape=jax.ShapeDtypeStruct((n_exp, bsz, n_slots, half), U32),
        scratch_shapes=[pltpu.VMEM((n_slots, half), U32), pltpu.SemaphoreType.DMA(())],
        compiler_params=_cparams(("arbitrary", "arbitrary"), 32),
        name="moe_gather",
    )(rows, b_pack)


def _expert_up_kernel(x_ref, wg_ref, wu_ref, o_ref, xlo_sc, xhi_sc):
    half = x_ref.shape[1]

    @pl.when(pl.program_id(1) == 0)
    def _unpack():
        w = x_ref[...]
        xlo_sc[...] = pltpu.bitcast(lax.shift_left(w, jnp.uint32(16)), F32).astype(BF16)
        xhi_sc[...] = pltpu.bitcast(w & jnp.uint32(0xFFFF0000), F32).astype(BF16)

    xlo = xlo_sc[...]
    xhi = xhi_sc[...]
    hg = _dot(xlo, wg_ref[0:half, :].astype(BF16)) + _dot(xhi, wg_ref[half:, :].astype(BF16))
    hu = _dot(xlo, wu_ref[0:half, :].astype(BF16)) + _dot(xhi, wu_ref[half:, :].astype(BF16))
    o_ref[...] = (hg * _sigmoid(hg) * hu).astype(BF16)


def _expert_down_kernel(h_ref, wd_ref, o_ref):
    o_ref[...] = _dot(h_ref[...], wd_ref[...].astype(BF16)).astype(BF16)


EXPERT_UP_COLS = 256
EXPERT_DOWN_COLS = 512


def _experts(xs, w_gate, w_up, w_down, layer):
    n_exp, rows, half = xs.shape
    d = 2 * half
    ff = w_gate.shape[3]
    tf = EXPERT_UP_COLS
    hid = pl.pallas_call(
        _expert_up_kernel,
        grid=(n_exp, ff // tf),
        in_specs=[
            pl.BlockSpec((None, rows, half), lambda e, f: (e, 0, 0)),
            pl.BlockSpec((None, None, d, tf), lambda e, f: (layer, e, 0, f)),
            pl.BlockSpec((None, None, d, tf), lambda e, f: (layer, e, 0, f)),
        ],
        out_specs=pl.BlockSpec((None, rows, tf), lambda e, f: (e, 0, f)),
        out_shape=jax.ShapeDtypeStruct((n_exp, rows, ff), BF16),
        scratch_shapes=[pltpu.VMEM((rows, half), BF16), pltpu.VMEM((rows, half), BF16)],
        compiler_params=_cparams(("arbitrary", "arbitrary"), 48),
        name="moe_expert_up",
    )(xs, w_gate, w_up)
    tn = EXPERT_DOWN_COLS
    return pl.pallas_call(
        _expert_down_kernel,
        grid=(n_exp, d // tn),
        in_specs=[
            pl.BlockSpec((None, rows, ff), lambda e, n: (e, 0, 0)),
            pl.BlockSpec((None, None, ff, tn), lambda e, n: (layer, e, 0, n)),
        ],
        out_specs=pl.BlockSpec((None, rows, tn), lambda e, n: (e, 0, n)),
        out_shape=jax.ShapeDtypeStruct((n_exp, rows, d), BF16),
        compiler_params=_cparams(("arbitrary", "arbitrary"), 48),
        name="moe_expert_down",
    )(hid, w_down)


SCATTER_WINDOW = 256
SCATTER_EXPERTS = 2
SCATTER_TILES = 4


def _scatter_kernel(base_ref, keyt_ref, afft_ref, ys_ref, h_ref, gl_ref, gc_ref, gf_ref, o_ref,
                    *, n_slots, ctx_len, tm, n_tiles, final_norm):
    b = pl.program_id(0)
    t = pl.program_id(1)
    eg = pl.program_id(2)
    n_exp = keyt_ref.shape[1]
    win = min(SCATTER_WINDOW, n_slots)
    lane = lax.broadcasted_iota(I32, (tm, n_exp), 1)
    keyt = keyt_ref[...].astype(F32)
    afft = afft_ref[...]
    @pl.when(eg == 0)
    def _zero():
        o_ref[...] = jnp.zeros_like(o_ref)

    total = o_ref[...]
    cols = []
    for k in range(SCATTER_EXPERTS):
        e = eg * SCATTER_EXPERTS + k
        pick = lane == e
        kcol = jnp.sum(jnp.where(pick, keyt, 0.0), axis=1, keepdims=True)
        gcol = jnp.sum(jnp.where(pick, afft, 0.0), axis=1, keepdims=True)
        lo = base_ref[(b * (n_tiles + 1) + t) * n_exp + e]
        hi = base_ref[(b * (n_tiles + 1) + t + 1) * n_exp + e]
        start = jnp.minimum((lo // (2 * SUBLANES)) * (2 * SUBLANES), n_slots - win)
        start = pl.multiple_of(start, 2 * SUBLANES)
        slot = (start + lax.broadcasted_iota(I32, (1, win), 1)).astype(F32)
        weights = jnp.where(kcol == slot, gcol, 0.0).astype(BF16)
        part = _dot(weights, ys_ref[k, pl.ds(start, win), :])
        total = total + part
        cols.append((kcol, gcol, start, hi))
    o_ref[...] = total

    for k, (kcol, gcol, start, hi) in enumerate(cols):
        @pl.when(hi > start + win)
        def _beyond_window():
            slot = lax.broadcasted_iota(I32, (1, n_slots), 1)
            outside = jnp.logical_or(slot < start, slot >= start + win)
            hit = jnp.logical_and(kcol == slot.astype(F32), outside)
            weights = jnp.where(hit, gcol, 0.0).astype(BF16)
            o_ref[...] += _dot(weights, ys_ref[k])

    @pl.when(eg == pl.num_programs(2) - 1)
    def _store():
        row = t * tm + lax.broadcasted_iota(I32, (tm, 1), 0)
        gate = jnp.where(row < ctx_len, gc_ref[...], gl_ref[...])
        hn = h_ref[...] + gate * o_ref[...]
        if final_norm:
            hn = _rms(hn, gf_ref[...])
        o_ref[...] = hn


def _scatter(base, key_t, aff_tt, ys, h, mods3, gfinal, n_slots, ctx_len, final_norm):
    bsz, L, d = h.shape
    n_exp = key_t.shape[2]
    n_tiles = SCATTER_TILES
    tm = L // n_tiles
    assert n_exp % SCATTER_EXPERTS == 0 and (n_slots - min(SCATTER_WINDOW, n_slots)) % (2 * SUBLANES) == 0
    kern = functools.partial(_scatter_kernel, n_slots=n_slots, ctx_len=ctx_len, tm=tm, n_tiles=n_tiles,
                             final_norm=final_norm)
    grid_spec = pltpu.PrefetchScalarGridSpec(
        num_scalar_prefetch=1,
        grid=(bsz, n_tiles, n_exp // SCATTER_EXPERTS),
        in_specs=[
            pl.BlockSpec((None, tm, n_exp), lambda b, t, e, *_: (b, t, 0)),
            pl.BlockSpec((None, tm, n_exp), lambda b, t, e, *_: (b, t, 0)),
            pl.BlockSpec((SCATTER_EXPERTS, None, n_slots, d), lambda b, t, e, *_: (e, b, 0, 0)),
            pl.BlockSpec((None, tm, d), lambda b, t, e, *_: (b, t, 0), pipeline_mode=pl.Buffered(1)),
            pl.BlockSpec((None, 1, d), lambda b, t, e, *_: (b, 0, 5)),
            pl.BlockSpec((None, 1, d), lambda b, t, e, *_: (bsz, 0, 5)),
            pl.BlockSpec((1, d), lambda b, t, e, *_: (0, 0)),
        ],
        out_specs=pl.BlockSpec((None, tm, d), lambda b, t, e, *_: (b, t, 0)),
    )
    return pl.pallas_call(
        kern,
        grid_spec=grid_spec,
        out_shape=jax.ShapeDtypeStruct((bsz, L, d), F32),
        compiler_params=_cparams(("arbitrary", "arbitrary", "arbitrary"), 56),
        name="moe_scatter",
    )(base, key_t, aff_tt, ys, h, mods3, mods3, gfinal)


def _rope_tables(ctx_len, seq):
    pos = np.arange(seq)
    freqs = ROPE_THETA ** (-jnp.arange(0, ROT_AXIS, 2, dtype=F32) / ROT_AXIS)

    def cs(p):
        ang = jnp.asarray(p, F32)[:, None] * freqs[None, :]
        return jnp.cos(ang), jnp.sin(ang)

    cr, sr = cs(pos // GRID_W)
    cc, sc = cs(pos % GRID_W)
    cos_l = jnp.concatenate([cr, cr, cc, cc], axis=1)
    sin_l = jnp.concatenate([-sr, sr, -sc, sc], axis=1)
    cos_t = jnp.concatenate([jnp.ones((ctx_len, HEAD_DIM), F32), cos_l], axis=0)
    sin_t = jnp.concatenate([jnp.zeros((ctx_len, HEAD_DIM), F32), sin_l], axis=0)
    return cos_t, sin_t


def _moe(h, b_act, aff_t, mods3, w_gate, w_up, w_down, layer, gfinal, ctx_len, moe_ctx, final_norm):
    bsz, L, d = h.shape
    seq = L - ctx_len
    n_exp = aff_t.shape[1]
    cap_l = max(1, EC_FACTOR * seq // n_exp)
    cap_c = max(1, EC_FACTOR * ctx_len // n_exp) if moe_ctx else 0
    n_slots = cap_l + cap_c
    segments = ((0, ctx_len, cap_c, 0), (ctx_len, seq, cap_l, cap_c))
    key, base, tok = _route(aff_t, segments, L // SCATTER_TILES, n_slots)
    rows = tok[:, :, :n_exp].transpose(2, 0, 1) + (jnp.arange(bsz, dtype=I32) * L)[None, :, None]
    xs = _gather(rows.reshape(n_exp, bsz, 1, n_slots), b_act.reshape(bsz * L, d // 2), n_slots)
    ys = _experts(xs.reshape(n_exp, bsz * n_slots, d // 2), w_gate, w_up, w_down, layer)
    ys = ys.reshape(n_exp, bsz, n_slots, d)
    base_flat = base[:, :, :SCATTER_TILES + 1].transpose(0, 2, 1).reshape(-1)
    return _scatter(base_flat, key.transpose(0, 2, 1), aff_t.transpose(0, 2, 1), ys, h, mods3, gfinal,
                    n_slots, ctx_len, final_norm)


def kernel(x, c, ctx, c_ctx, mod_w, mod_b, norm1_g, norm2_g, ev_w_in, ev_q_norm, ev_k_norm, ev_conv_w, ev_conv_b, ev_w_out, od_w_in, od_conv_w, od_conv_b, od_wa, od_ba, od_wx, od_bx, od_lam, od_w_out, moe_router, moe_w_gate, moe_w_up, moe_w_down, final_norm_g):
    bsz, seq, d = x.shape
    ctx_len = ctx.shape[1]
    depth = mod_w.shape[0]
    assert bsz < SUBLANES and ctx_len % ROW_TILE == 0 and seq % (2 * KV_CHUNK) == 0
    attn_w = d // 2
    kv_w = attn_w // Q_PER_KV
    conv_w = d // 2
    lru_w = od_w_in.shape[2] // 2

    h = (ctx, x)
    cvec = jnp.concatenate([c, c_ctx[None, :], jnp.zeros((SUBLANES - bsz - 1, d), F32)], axis=0)
    mods = _mods(cvec, mod_w, mod_b)
    cos_t, sin_t = _rope_tables(ctx_len, seq)
    gfinal = final_norm_g.reshape(1, d)

    rows_ctx = ctx_len
    for i in range(depth):
        last = i == depth - 1
        mods3 = mods[i].reshape(SUBLANES, 1, 6 * d)
        g1 = norm1_g[i].reshape(1, d)
        g2 = norm2_g[i].reshape(1, d)
        wr = moe_router[i].T
        wr_hi = wr.astype(BF16)
        wr_lo = (wr - wr_hi.astype(F32)).astype(BF16)
        j = i // 2
        if i % 2 == 0:
            q, k, vt, gb, p = _even_in(h, mods3, g1, ev_w_in[j].astype(BF16),
                                      ev_q_norm[j].reshape(1, HEAD_DIM), ev_k_norm[j].reshape(1, HEAD_DIM),
                                      cos_t, sin_t, ctx_len, attn_w, kv_w, conv_w)
            att = _attention(q, k, vt, ctx_len)
            h, b_act, aff_t = _even_out(h, att, gb, p, ev_conv_w[j], ev_conv_b[j].reshape(1, conv_w),
                                        ev_w_out[j].astype(BF16), mods3, g2, wr_hi, wr_lo, ctx_len)
        else:
            u, gg, edge = _odd_in(h, mods3, g1, od_w_in[j].astype(BF16), od_conv_w[j],
                                  od_conv_b[j].reshape(1, lru_w), ctx_len, lru_w)
            y2 = _lru(u, edge, od_conv_w[j], od_wa[j].astype(BF16), od_wx[j].astype(BF16),
                      od_ba[j].reshape(2, 1, lru_w), od_bx[j].reshape(2, 1, lru_w),
                      od_lam[j].reshape(2, 1, lru_w), ctx_len)
            h, b_act, aff_t = _odd_out(h, y2, gg, od_w_out[j].astype(BF16), mods3, g2, wr_hi, wr_lo, ctx_len,
                                       latent_only=last)
            if last:
                rows_ctx = 0
        h = _moe(h, b_act, aff_t, mods3, moe_w_gate, moe_w_up, moe_w_down, i, gfinal,
                 rows_ctx, moe_ctx=not last, final_norm=last)
    return h[:, rows_ctx:, :]
```

```python
import functools

import jax
import jax.numpy as jnp
import numpy as np
from jax import lax
from jax.experimental import pallas as pl
from jax.experimental.pallas import tpu as pltpu

F32 = jnp.float32
BF16 = jnp.bfloat16
I32 = jnp.int32
U32 = jnp.uint32

EPS = 1e-6
HEAD_DIM = 128
ROT_AXIS = HEAD_DIM // 2
ROPE_THETA = 10000.0
GRID_W = 64
Q_PER_KV = 4
LRU_BLOCK = 256
LRU_C = 8.0
N_EXPERTS = 16
EC_FACTOR = 2

LANES = 128
SUBLANES = 8
ROW_TILE = 256
KV_CHUNK = 512
MIB = 1024 * 1024


def _cparams(sem, vmem_mib):
    return pltpu.CompilerParams(dimension_semantics=sem, vmem_limit_bytes=int(vmem_mib * MIB))


def _sigmoid(x):
    return 1.0 / (1.0 + jnp.exp(-x))


def _rms(x, g):
    return x * lax.rsqrt(jnp.mean(x * x, axis=-1, keepdims=True) + EPS) * g


def _dot(a, b):
    return jnp.dot(a, b, preferred_element_type=F32)


def _dot_nt(a, b):
    return lax.dot_general(a, b, (((1,), (1,)), ((), ())), preferred_element_type=F32)


def _resident(shape):
    nd = len(shape)
    return pl.BlockSpec(shape, lambda *_: (0,) * nd)


def _mods_kernel(c_ref, w_ref, b_ref, o_ref):
    c = c_ref[...]
    s = (c * _sigmoid(c)).astype(BF16)
    o_ref[...] = _dot(s, w_ref[...].astype(BF16)) + b_ref[...]


def _mods(cvec, mod_w, mod_b):
    depth, d, n6 = mod_w.shape
    tn = 1536
    return pl.pallas_call(
        _mods_kernel,
        grid=(depth, n6 // tn),
        in_specs=[
            pl.BlockSpec((SUBLANES, d), lambda l, j: (0, 0)),
            pl.BlockSpec((None, d, tn), lambda l, j: (l, 0, j)),
            pl.BlockSpec((None, 1, tn), lambda l, j: (l, 0, j)),
        ],
        out_specs=pl.BlockSpec((None, SUBLANES, tn), lambda l, j: (l, 0, j)),
        out_shape=jax.ShapeDtypeStruct((depth, SUBLANES, n6), F32),
        compiler_params=_cparams(("arbitrary", "arbitrary"), 40),
        name="adaln_mods",
    )(cvec, mod_w, mod_b.reshape(depth, 1, n6))


def _mod_spec(bsz, n_ctx_tiles, d, k):
    return pl.BlockSpec((None, 1, d), lambda b, i: (jnp.where(i < n_ctx_tiles, bsz, b), 0, k))


def _stream_parts(h):
    return tuple(h) if isinstance(h, (tuple, list)) else (h,)


def _stream_specs(parts, tm, n_ctx_tiles):
    d = parts[0].shape[2]
    if len(parts) == 1:
        return [pl.BlockSpec((None, tm, d), lambda b, i: (b, i, 0))]
    return [pl.BlockSpec((None, tm, d), lambda b, i: (b, jnp.minimum(i, n_ctx_tiles - 1), 0)),
            pl.BlockSpec((None, tm, d), lambda b, i: (b, jnp.maximum(i - n_ctx_tiles, 0), 0))]


def _stream_tile(refs, tile, n_ctx_tiles):
    if len(refs) == 1:
        return refs[0][...]
    return jnp.where(tile < n_ctx_tiles, refs[0][...], refs[1][...])


def _even_in_kernel(*refs, n_parts, n_ctx_tiles, attn_w, kv_w, conv_w):
    (g1_ref, sh_ref, sc_ref, w_ref, qg_ref, kg_ref, cos_ref, sin_ref,
     q_ref, k_ref, v_ref, gb_ref, p_ref) = refs[n_parts:]
    x = _stream_tile(refs[:n_parts], pl.program_id(1), n_ctx_tiles)
    a = (_rms(x, g1_ref[...]) * (1.0 + sc_ref[...]) + sh_ref[...]).astype(BF16)
    cos = cos_ref[...]
    sin = sin_ref[...]
    tm = x.shape[0]
    lane = lax.broadcasted_iota(I32, (tm, HEAD_DIM), 1)
    low_half = (lane % ROT_AXIS) < (ROT_AXIS // 2)

    def norm_rope(t, g, scale):
        t = t * lax.rsqrt(jnp.mean(t * t, axis=-1, keepdims=True) + EPS) * g
        swapped = jnp.where(low_half, pltpu.roll(t, HEAD_DIM - ROT_AXIS // 2, 1),
                            pltpu.roll(t, ROT_AXIS // 2, 1))
        return (t * cos + swapped * sin) * scale

    cw = 4 * HEAD_DIM
    qg = qg_ref[...]
    kg = kg_ref[...]
    q_scale = HEAD_DIM ** -0.5 * np.log2(np.e)
    for c0 in range(0, attn_w, cw):
        y = _dot(a, w_ref[:, c0:c0 + cw])
        for j in range(cw // HEAD_DIM):
            t = norm_rope(y[:, j * HEAD_DIM:(j + 1) * HEAD_DIM], qg, q_scale)
            q_ref[:, c0 + j * HEAD_DIM:c0 + (j + 1) * HEAD_DIM] = t.astype(BF16)
    y = _dot(a, w_ref[:, attn_w:attn_w + 2 * kv_w])
    for j in range(kv_w // HEAD_DIM):
        t = norm_rope(y[:, j * HEAD_DIM:(j + 1) * HEAD_DIM], kg, 1.0)
        k_ref[:, j * HEAD_DIM:(j + 1) * HEAD_DIM] = t.astype(BF16)
    v_ref[...] = y[:, kv_w:2 * kv_w].T.astype(BF16)
    off_b = attn_w + 2 * kv_w
    off_c = off_b + conv_w
    off_u = off_c + conv_w
    for c0 in range(0, conv_w, cw):
        gb_ref[:, c0:c0 + cw] = _dot(a, w_ref[:, off_b + c0:off_b + c0 + cw]).astype(BF16)
        cc = _dot(a, w_ref[:, off_c + c0:off_c + c0 + cw])
        uu = _dot(a, w_ref[:, off_u + c0:off_u + c0 + cw])
        p_ref[:, c0:c0 + cw] = (cc * uu).astype(BF16)


def _even_in(h, mods3, g1, w_in, qg, kg, cos_t, sin_t, ctx_len, attn_w, kv_w, conv_w):
    parts = _stream_parts(h)
    bsz, _, d = parts[0].shape
    L = sum(part.shape[1] for part in parts)
    tm = ROW_TILE
    nct = ctx_len // tm
    tok = lambda width: pl.BlockSpec((None, tm, width), lambda b, i: (b, i, 0))
    kern = functools.partial(_even_in_kernel, n_parts=len(parts), n_ctx_tiles=nct,
                             attn_w=attn_w, kv_w=kv_w, conv_w=conv_w)
    return pl.pallas_call(
        kern,
        grid=(bsz, L // tm),
        in_specs=_stream_specs(parts, tm, nct) + [
            _resident((1, d)),
            _mod_spec(bsz, nct, d, 0),
            _mod_spec(bsz, nct, d, 1),
            _resident(w_in.shape),
            _resident((1, HEAD_DIM)),
            _resident((1, HEAD_DIM)),
            pl.BlockSpec((tm, HEAD_DIM), lambda b, i: (i, 0)),
            pl.BlockSpec((tm, HEAD_DIM), lambda b, i: (i, 0)),
        ],
        out_specs=[tok(attn_w), tok(kv_w), pl.BlockSpec((None, kv_w, tm), lambda b, i: (b, 0, i)),
                   tok(conv_w), tok(conv_w)],
        out_shape=[
            jax.ShapeDtypeStruct((bsz, L, attn_w), BF16),
            jax.ShapeDtypeStruct((bsz, L, kv_w), BF16),
            jax.ShapeDtypeStruct((bsz, kv_w, L), BF16),
            jax.ShapeDtypeStruct((bsz, L, conv_w), BF16),
            jax.ShapeDtypeStruct((bsz, L, conv_w), BF16),
        ],
        compiler_params=_cparams(("arbitrary", "arbitrary"), 56),
        name="even_in_proj",
    )(*parts, g1, mods3, mods3, w_in, qg, kg, cos_t, sin_t)


def _attn_kernel(q_ref, k_ref, vt_ref, o_ref, m_sc, l_sc, acc_sc, sa_sc, sb_sc, *, ctx_len, n_lat_chunks, tq):
    qi = pl.program_id(2)
    q = q_ref[...]
    qs = jnp.concatenate([q[:, j * HEAD_DIM:(j + 1) * HEAD_DIM] for j in range(Q_PER_KV)], axis=0)

    def lat_start(c):
        return pl.multiple_of(ctx_len + c * KV_CHUNK, int(np.gcd(ctx_len, KV_CHUNK)))

    def scores(c):
        return _dot_nt(k_ref[pl.ds(lat_start(c), KV_CHUNK), :], qs)

    def absorb(s_ref, c):
        sc = s_ref[...]
        m_old = m_sc[...]
        m_new = jnp.maximum(m_old, jnp.max(sc, axis=0, keepdims=True))
        alpha = jnp.exp2(m_old - m_new)
        pc = jnp.exp2(sc - m_new)
        l_sc[...] = alpha * l_sc[...] + jnp.sum(pc, axis=0, keepdims=True)
        acc_sc[...] = alpha * acc_sc[...] + _dot(vt_ref[:, pl.ds(lat_start(c), KV_CHUNK)], pc.astype(BF16))
        m_sc[...] = m_new

    s = _dot_nt(k_ref[0:ctx_len, :], qs)
    m0 = jnp.max(s, axis=0, keepdims=True)
    p = jnp.exp2(s - m0)
    m_sc[...] = m0
    l_sc[...] = jnp.sum(p, axis=0, keepdims=True)
    acc_sc[...] = _dot(vt_ref[:, 0:ctx_len], p.astype(BF16))

    @pl.when(qi >= ctx_len // tq)
    def _latent_keys():
        sa_sc[...] = scores(0)

        def body(i, carry):
            c = 2 * i
            sb_sc[...] = scores(c + 1)
            absorb(sa_sc, c)
            sa_sc[...] = scores(jnp.minimum(c + 2, n_lat_chunks - 1))
            absorb(sb_sc, c + 1)
            return carry
        lax.fori_loop(0, n_lat_chunks // 2, body, 0)

    o = acc_sc[...] * (1.0 / l_sc[...])
    for j in range(Q_PER_KV):
        o_ref[:, j * HEAD_DIM:(j + 1) * HEAD_DIM] = o[:, j * tq:(j + 1) * tq].T.astype(BF16)


def _attention(q, k, vt, ctx_len):
    bsz, L, attn_w = q.shape
    n_kv = k.shape[2] // HEAD_DIM
    tq = ROW_TILE
    gw = Q_PER_KV * HEAD_DIM
    seq = L - ctx_len
    kern = functools.partial(_attn_kernel, ctx_len=ctx_len, n_lat_chunks=seq // KV_CHUNK, tq=tq)
    return pl.pallas_call(
        kern,
        grid=(bsz, n_kv, L // tq),
        in_specs=[
            pl.BlockSpec((None, tq, gw), lambda b, g, i: (b, i, g)),
            pl.BlockSpec((None, L, HEAD_DIM), lambda b, g, i: (b, 0, g)),
            pl.BlockSpec((None, HEAD_DIM, L), lambda b, g, i: (b, g, 0)),
        ],
        out_specs=pl.BlockSpec((None, tq, gw), lambda b, g, i: (b, i, g)),
        out_shape=jax.ShapeDtypeStruct((bsz, L, attn_w), BF16),
        scratch_shapes=[
            pltpu.VMEM((1, Q_PER_KV * tq), F32),
            pltpu.VMEM((1, Q_PER_KV * tq), F32),
            pltpu.VMEM((HEAD_DIM, Q_PER_KV * tq), F32),
            pltpu.VMEM((KV_CHUNK, Q_PER_KV * tq), F32),
            pltpu.VMEM((KV_CHUNK, Q_PER_KV * tq), F32),
        ],
        compiler_params=_cparams(("arbitrary", "arbitrary", "arbitrary"), 40),
        name="attention",
    )(q, k, vt)


def _gelu_tanh(x):
    return 0.5 * x * (1.0 + jnp.tanh(np.sqrt(2.0 / np.pi) * (x + 0.044715 * (x * x * x))))


def _odd_in_kernel(h_ref, g1_ref, sh_ref, sc_ref, w_ref, cw_ref, cb_ref, u_ref, gg_ref, edge_ref, *, lru_w):
    x = h_ref[...]
    a = (_rms(x, g1_ref[...]) * (1.0 + sc_ref[...]) + sh_ref[...]).astype(BF16)
    cw = 512
    tm = x.shape[0]
    sub8 = lax.broadcasted_iota(I32, (SUBLANES, 1), 0)
    for c0 in range(0, lru_w, cw):
        cs = slice(c0, c0 + cw)
        y = _dot(a, w_ref[:, cs])
        w0, w1, w2, w3 = (cw_ref[k:k + 1, cs] for k in range(4))
        u = (pltpu.roll(y, 1, 0) * w0 + y * w1 + pltpu.roll(y, tm - 1, 0) * w2 + pltpu.roll(y, tm - 2, 0) * w3
             + cb_ref[:, cs])
        wrap_first = jnp.where(sub8 == 0, y[tm - 1:tm, :] * w0, 0.0)
        wrap_last = jnp.where(sub8 == SUBLANES - 1, y[0:1, :] * w2 + y[1:2, :] * w3,
                              jnp.where(sub8 == SUBLANES - 2, y[0:1, :] * w3, 0.0))
        u_ref[0:SUBLANES, cs] = u[0:SUBLANES] - wrap_first
        u_ref[SUBLANES:tm - SUBLANES, cs] = u[SUBLANES:tm - SUBLANES]
        u_ref[tm - SUBLANES:tm, cs] = u[tm - SUBLANES:] - wrap_last
        edge_ref[:, cs] = jnp.where(sub8 == 0, y[0:1, :], jnp.where(sub8 == 1, y[1:2, :],
                                    jnp.where(sub8 == 2, y[tm - 1:tm, :], 0.0)))
        gg_ref[:, cs] = _gelu_tanh(_dot(a, w_ref[:, lru_w + c0:lru_w + c0 + cw])).astype(BF16)


def _odd_in(h, mods3, g1, w_in, conv_w, conv_b, ctx_len, lru_w):
    bsz, L, d = h.shape
    tm = ROW_TILE
    nct = ctx_len // tm
    tok = lambda width: pl.BlockSpec((None, tm, width), lambda b, i: (b, i, 0))
    return pl.pallas_call(
        functools.partial(_odd_in_kernel, lru_w=lru_w),
        grid=(bsz, L // tm),
        in_specs=[tok(d), _resident((1, d)), _mod_spec(bsz, nct, d, 0), _mod_spec(bsz, nct, d, 1),
                  _resident(w_in.shape), _resident(conv_w.shape), _resident((1, lru_w))],
        out_specs=[tok(lru_w), tok(lru_w),
                   pl.BlockSpec((None, None, SUBLANES, lru_w), lambda b, i: (b, i, 0, 0))],
        out_shape=[jax.ShapeDtypeStruct((bsz, L, lru_w), F32),
                   jax.ShapeDtypeStruct((bsz, L, lru_w), BF16),
                   jax.ShapeDtypeStruct((bsz, L // tm, SUBLANES, lru_w), F32)],
        compiler_params=_cparams(("arbitrary", "arbitrary"), 56),
        name="odd_in_proj",
    )(h, g1, mods3, mods3, w_in, conv_w, conv_b)


def _lru_tile(d, j, n_tiles, n_ctx_tiles):
    rev = jnp.where(j < n_ctx_tiles, n_ctx_tiles - 1 - j, n_tiles - 1 - (j - n_ctx_tiles))
    return jnp.where(d == 0, j, rev)


def _lru_kernel(u_ref, ep_ref, en_ref, cw_ref, wa_ref, wx_ref, ba_ref, bx_ref, lam_ref,
                y_ref, a_sc, b_sc, y_sc, h_sc, *, n_tiles, n_ctx_tiles, tt):
    d = pl.program_id(1)
    j = pl.program_id(2)
    tile = _lru_tile(d, j, n_tiles, n_ctx_tiles)
    width = u_ref.shape[1]
    seg_first = jnp.logical_or(tile == 0, tile == n_ctx_tiles)
    seg_last = jnp.logical_or(tile == n_ctx_tiles - 1, tile == n_tiles - 1)
    prev = jnp.where(seg_first, 0.0, ep_ref[2:3, :])
    nxt1 = jnp.where(seg_last, 0.0, en_ref[0:1, :])
    nxt2 = jnp.where(seg_last, 0.0, en_ref[1:2, :])
    w0, w2, w3 = cw_ref[0:1, :], cw_ref[2:3, :], cw_ref[3:4, :]
    sub8 = lax.broadcasted_iota(I32, (SUBLANES, 1), 0)
    add_first = jnp.where(sub8 == 0, prev * w0, 0.0)
    add_last = jnp.where(sub8 == SUBLANES - 1, nxt1 * w2 + nxt2 * w3,
                         jnp.where(sub8 == SUBLANES - 2, nxt1 * w3, 0.0))
    u = jnp.concatenate([u_ref[0:SUBLANES, :] + add_first, u_ref[SUBLANES:tt - SUBLANES, :],
                         u_ref[tt - SUBLANES:tt, :] + add_last], axis=0)
    z = -lam_ref[...]
    la = (-LRU_C * np.log2(np.e)) * (jnp.maximum(z, 0.0) + jnp.log(1.0 + jnp.exp(-jnp.abs(z))))
    for blk in range(width // LRU_BLOCK):
        cs = slice(blk * LRU_BLOCK, (blk + 1) * LRU_BLOCK)
        ub = u[:, cs]
        ub16 = ub.astype(BF16)
        r = _sigmoid(_dot(ub16, wa_ref[blk]) + ba_ref[:, cs])
        ig = _sigmoid(_dot(ub16, wx_ref[blk]) + bx_ref[:, cs])
        a = jnp.exp2(r * la[:, cs])
        a_sc[:, cs] = a
        om = 1.0 - a * a
        b_sc[:, cs] = jnp.where(om > 0.0, om * lax.rsqrt(om), 0.0) * ig * ub

    @pl.when(j == 0)
    def _reset():
        h_sc[...] = jnp.zeros_like(h_sc)

    sub = lax.broadcasted_iota(I32, (SUBLANES, width), 0)
    n_groups = tt // SUBLANES

    def group_scan(base, forward):
        av = a_sc[pl.ds(base, SUBLANES), :]
        bv = b_sc[pl.ds(base, SUBLANES), :]
        for s in (1, 2, 4):
            shift = s if forward else SUBLANES - s
            keep = (sub >= s) if forward else (sub < SUBLANES - s)
            a_sh = pltpu.roll(av, shift, 0)
            b_sh = pltpu.roll(bv, shift, 0)
            bv = jnp.where(keep, av * b_sh + bv, bv)
            av = jnp.where(keep, av * a_sh, av)
        out = av * h_sc[...] + bv
        y_sc[pl.ds(base, SUBLANES), :] = out
        last = out[SUBLANES - 1:SUBLANES, :] if forward else out[0:1, :]
        h_sc[...] = jnp.broadcast_to(last, (SUBLANES, width))

    @pl.when(d == 0)
    def _forward():
        def body(g, carry):
            group_scan(pl.multiple_of(g * SUBLANES, SUBLANES), True)
            return carry
        lax.fori_loop(0, n_groups, body, 0)

    @pl.when(d == 1)
    def _backward():
        def body(g, carry):
            group_scan(pl.multiple_of((n_groups - 1 - g) * SUBLANES, SUBLANES), False)
            return carry
        lax.fori_loop(0, n_groups, body, 0)

    y_ref[...] = y_sc[...].astype(BF16)


def _lru(u, edge, conv_w, wa, wx, ba, bx, lam, ctx_len):
    bsz, L, width = u.shape
    tt = ROW_TILE
    nt = L // tt
    nct = ctx_len // tt
    nb = width // LRU_BLOCK
    tile = lambda d, j: _lru_tile(d, j, nt, nct)
    kern = functools.partial(_lru_kernel, n_tiles=nt, n_ctx_tiles=nct, tt=tt)
    dir_vec = pl.BlockSpec((None, 1, width), lambda b, d, j: (d, 0, 0))
    dir_mat = pl.BlockSpec((None, nb, LRU_BLOCK, LRU_BLOCK), lambda b, d, j: (d, 0, 0, 0))
    return pl.pallas_call(
        kern,
        grid=(bsz, 2, nt),
        in_specs=[
            pl.BlockSpec((None, tt, width), lambda b, d, j: (b, tile(d, j), 0)),
            pl.BlockSpec((None, None, SUBLANES, width),
                         lambda b, d, j: (b, jnp.maximum(tile(d, j) - 1, 0), 0, 0)),
            pl.BlockSpec((None, None, SUBLANES, width),
                         lambda b, d, j: (b, jnp.minimum(tile(d, j) + 1, nt - 1), 0, 0)),
            pl.BlockSpec(conv_w.shape, lambda b, d, j: (0, 0)),
            dir_mat, dir_mat, dir_vec, dir_vec, dir_vec,
        ],
        out_specs=pl.BlockSpec((None, None, tt, width), lambda b, d, j: (d, b, tile(d, j), 0)),
        out_shape=jax.ShapeDtypeStruct((2, bsz, L, width), BF16),
        scratch_shapes=[pltpu.VMEM((tt, width), F32), pltpu.VMEM((tt, width), F32),
                        pltpu.VMEM((tt, width), F32), pltpu.VMEM((SUBLANES, width), F32)],
        compiler_params=_cparams(("arbitrary", "arbitrary", "arbitrary"), 48),
        name="rglru_scan",
    )(u, edge, edge, conv_w, wa, wx, ba, bx, lam)


def _finish_tile(hn, g2_ref, sh2_ref, sc2_ref, wrh_ref, wrl_ref, ho_ref, bo_ref, aff_ref):
    ho_ref[...] = hn
    bn = _rms(hn, g2_ref[...]) * (1.0 + sc2_ref[...]) + sh2_ref[...]
    b_hi = bn.astype(BF16)
    half = bn.shape[1] // 2
    lo = lax.shift_right_logical(pltpu.bitcast(b_hi[:, :half].astype(F32), U32), jnp.uint32(16))
    hi = pltpu.bitcast(b_hi[:, half:].astype(F32), U32) & jnp.uint32(0xFFFF0000)
    bo_ref[...] = lo | hi
    b_lo = (bn - b_hi.astype(F32)).astype(BF16)
    wrh = wrh_ref[...]
    logits = _dot_nt(wrh, b_hi) + _dot_nt(wrh, b_lo) + _dot_nt(wrl_ref[...], b_hi)
    ex = jnp.exp(logits - jnp.max(logits, axis=0, keepdims=True))
    aff_ref[...] = ex / jnp.sum(ex, axis=0, keepdims=True)


def _pipeline_slots(hn_sc):
    s = pl.program_id(0)
    slot = lax.rem(s, 2)

    @pl.when(s == 0)
    def _no_previous_tile():
        hn_sc[1] = jnp.zeros(hn_sc.shape[1:], F32)

    return 1 - slot, slot


def _even_out_kernel(*refs, n_parts, n_tiles, n_ctx_tiles, n_steps, attn_w):
    (att_ref, gb_ref, p_ref, pp_ref, pn_ref, cw_ref, cb_ref, w_ref,
     gate_ref, g2_ref, sh2_ref, sc2_ref, wrh_ref, wrl_ref, ho_ref, bo_ref, aff_ref, hn_sc) = refs[n_parts:]
    rd, wr = _pipeline_slots(hn_sc)
    _finish_tile(hn_sc[rd], g2_ref, sh2_ref, sc2_ref, wrh_ref, wrl_ref, ho_ref, bo_ref, aff_ref)

    i = lax.rem(jnp.minimum(pl.program_id(0), n_steps - 1), n_tiles)
    h = _stream_tile(refs[:n_parts], i, n_ctx_tiles)
    tm = p_ref.shape[0]
    p = p_ref[...].astype(F32)
    seg_first = jnp.logical_or(i == 0, i == n_ctx_tiles)
    seg_last = jnp.logical_or(i == n_ctx_tiles - 1, i == n_tiles - 1)
    prev = jnp.where(seg_first, 0.0, pp_ref[...].astype(F32)[2 * SUBLANES - 1:2 * SUBLANES, :])
    nxt = jnp.where(seg_last, 0.0, pn_ref[...].astype(F32)[0:1, :])
    row = lax.broadcasted_iota(I32, (tm, 1), 0)
    p_m1 = jnp.where(row == 0, prev, pltpu.roll(p, 1, 0))
    p_p1 = jnp.where(row == tm - 1, nxt, pltpu.roll(p, tm - 1, 0))
    conv = p_m1 * cw_ref[0:1, :] + p * cw_ref[1:2, :] + p_p1 * cw_ref[2:3, :] + cb_ref[...]
    act = (gb_ref[...].astype(F32) * conv).astype(BF16)
    y = _dot(att_ref[...], w_ref[0:attn_w, :]) + _dot(act, w_ref[attn_w:, :])
    hn_sc[wr] = h + gate_ref[...] * y


def _odd_out_kernel(h_ref, yf_ref, yr_ref, gg_ref, w_ref,
                    gate_ref, g2_ref, sh2_ref, sc2_ref, wrh_ref, wrl_ref,
                    ho_ref, bo_ref, aff_ref, hn_sc):
    rd, wr = _pipeline_slots(hn_sc)
    _finish_tile(hn_sc[rd], g2_ref, sh2_ref, sc2_ref, wrh_ref, wrl_ref, ho_ref, bo_ref, aff_ref)

    ysum = (yf_ref[...].astype(F32) + yr_ref[...].astype(F32)).astype(BF16)
    act = (ysum.astype(F32) * gg_ref[...].astype(F32)).astype(BF16)
    hn_sc[wr] = h_ref[...] + gate_ref[...] * _dot(act, w_ref[...])


def _remap(spec, tile_of_step):
    return pl.BlockSpec(spec.block_shape, lambda s: spec.index_map(*tile_of_step(s)))


def _out_call(kern, name, bsz, L_out, d, ctx_len, n_exp, tile_specs, const_specs, operands,
              mods3, g2, wr_hi, wr_lo):
    tm = ROW_TILE
    nt = L_out // tm
    nct = ctx_len // tm
    n_steps = bsz * nt

    def cur(s):
        sc = jnp.minimum(s, n_steps - 1)
        return sc // nt, lax.rem(sc, nt)

    def prv(s):
        sp = jnp.maximum(s - 1, 0)
        return sp // nt, lax.rem(sp, nt)

    in_specs = ([_remap(sp, cur) for sp in tile_specs] + const_specs
                + [_remap(_mod_spec(bsz, nct, d, 2), cur), _resident((1, d)),
                   _remap(_mod_spec(bsz, nct, d, 3), prv), _remap(_mod_spec(bsz, nct, d, 4), prv),
                   _resident((n_exp, d)), _resident((n_exp, d))])
    out_specs = [
        _remap(pl.BlockSpec((None, tm, d), lambda b, i: (b, i, 0)), prv),
        _remap(pl.BlockSpec((None, tm, d // 2), lambda b, i: (b, i, 0)), prv),
        _remap(pl.BlockSpec((None, n_exp, tm), lambda b, i: (b, 0, i)), prv),
    ]
    out_shape = [
        jax.ShapeDtypeStruct((bsz, L_out, d), F32),
        jax.ShapeDtypeStruct((bsz, L_out, d // 2), U32),
        jax.ShapeDtypeStruct((bsz, n_exp, L_out), F32),
    ]
    return pl.pallas_call(
        kern,
        grid=(n_steps + 1,),
        in_specs=in_specs,
        out_specs=out_specs,
        out_shape=out_shape,
        scratch_shapes=[pltpu.VMEM((2, tm, d), F32)],
        compiler_params=_cparams(("arbitrary",), 52),
        name=name,
    )(*operands, mods3, g2, mods3, mods3, wr_hi, wr_lo)


def _even_out(h, att, gb, p, conv_w, conv_b, w_out, mods3, g2, wr_hi, wr_lo, ctx_len):
    parts = _stream_parts(h)
    bsz, _, d = parts[0].shape
    L = sum(part.shape[1] for part in parts)
    n_exp = wr_hi.shape[0]
    tm = ROW_TILE
    nct = ctx_len // tm
    attn_w = att.shape[2]
    conv_wd = p.shape[2]
    halo = 2 * SUBLANES
    hb = tm // halo
    tok = lambda width: pl.BlockSpec((None, tm, width), lambda b, i: (b, i, 0))
    kern = functools.partial(_even_out_kernel, n_parts=len(parts), n_tiles=L // tm, n_ctx_tiles=nct,
                             n_steps=bsz * (L // tm), attn_w=attn_w)
    tile_specs = _stream_specs(parts, tm, nct) + [
        tok(attn_w), tok(conv_wd), tok(conv_wd),
        pl.BlockSpec((None, halo, conv_wd), lambda b, i: (b, jnp.maximum(i * hb - 1, 0), 0)),
        pl.BlockSpec((None, halo, conv_wd), lambda b, i: (b, jnp.minimum((i + 1) * hb, L // halo - 1), 0)),
    ]
    const_specs = [_resident(conv_w.shape), _resident((1, conv_wd)), _resident(w_out.shape)]
    return _out_call(kern, "even_out_proj", bsz, L, d, ctx_len, n_exp, tile_specs, const_specs,
                     (*parts, att, gb, p, p, p, conv_w, conv_b, w_out), mods3, g2, wr_hi, wr_lo)


def _odd_out(h, y2, gg, w_out, mods3, g2, wr_hi, wr_lo, ctx_len, latent_only):
    bsz, L, d = h.shape
    n_exp = wr_hi.shape[0]
    width = gg.shape[2]
    tm = ROW_TILE
    skip = ctx_len // tm if latent_only else 0
    tok = lambda wd: pl.BlockSpec((None, tm, wd), lambda b, i: (b, i + skip, 0))
    tile_specs = [
        tok(d),
        pl.BlockSpec((None, None, tm, width), lambda b, i: (0, b, i + skip, 0)),
        pl.BlockSpec((None, None, tm, width), lambda b, i: (1, b, i + skip, 0)),
        tok(width),
    ]
    return _out_call(_odd_out_kernel, "odd_out_proj", bsz, L - skip * tm, d, 0 if latent_only else ctx_len,
                     n_exp, tile_specs, [_resident(w_out.shape)], (h, y2, y2, gg, w_out),
                     mods3, g2, wr_hi, wr_lo)


def _route_kernel(aff_ref, key_ref, base_ref, tok_ref, lc_sc, *, segments, scatter_tile, n_slots):
    n_exp, L = aff_ref.shape
    for (start, n, cap, slot_off) in segments:
        if n == 0:
            continue
        if cap == 0:
            key_ref[:, start:start + n] = jnp.full((n_exp, n), -1, I32)
            continue
        aff = aff_ref[:, start:start + n]
        bits = pltpu.bitcast(aff, I32)
        capf = jnp.float32(cap)

        def count(mask):
            return jnp.sum(jnp.where(mask, 1.0, 0.0), axis=1, keepdims=True)

        thr = jnp.zeros((n_exp, 1), I32)
        for bit in range(30, -1, -1):
            cand = thr | jnp.int32(1 << bit)
            thr = jnp.where(count(bits >= cand) >= capf, cand, thr)
        gt = bits > thr
        eq = bits == thr
        need = capf - count(gt)

        cwid = 256 if n % 256 == 0 else LANES
        ri = lax.broadcasted_iota(I32, (cwid, cwid), 0)
        ci = lax.broadcasted_iota(I32, (cwid, cwid), 1)
        upper = jnp.where(ri <= ci, 1.0, 0.0).astype(BF16)

        def cumsum_excl(xf):
            outs = []
            carry = jnp.zeros((n_exp, 1), F32)
            for c in range(n // cwid):
                xc = xf[:, c * cwid:(c + 1) * cwid]
                inc = _dot(xc.astype(BF16), upper)
                outs.append(inc - xc + carry)
                carry = carry + inc[:, cwid - 1:cwid]
            return jnp.concatenate(outs, axis=1)

        eqf = jnp.where(eq, 1.0, 0.0)
        sel = jnp.logical_or(gt, jnp.logical_and(eq, cumsum_excl(eqf) < need))
        self_f = jnp.where(sel, 1.0, 0.0)
        slot = cumsum_excl(self_f) + jnp.float32(slot_off)
        key_ref[:, start:start + n] = jnp.where(sel, slot, -1.0).astype(I32)

    taken = jnp.where(key_ref[...] >= 0, 1.0, 0.0)
    tok = lax.broadcasted_iota(I32, (n_exp, L), 1)
    lane = lax.broadcasted_iota(I32, (n_exp, LANES), 1)
    base = jnp.zeros((n_exp, LANES), F32)
    for k in range(1, L // scatter_tile + 1):
        cnt = jnp.sum(jnp.where(tok < k * scatter_tile, taken, 0.0), axis=1, keepdims=True)
        base = jnp.where(lane == k, cnt, base)
    base_ref[...] = base.astype(I32)

    cw = ROUTE_CHUNK
    nch = L // cw
    ri = lax.broadcasted_iota(I32, (cw, cw), 0)
    ci = lax.broadcasted_iota(I32, (cw, cw), 1)
    upper = jnp.where(ri <= ci, 1.0, 0.0).astype(BF16)
    incs = []
    big = jnp.float32(2 ** 30)
    chunk_lo = jnp.full((n_exp, LANES), big, F32)
    chunk_hi = jnp.full((n_exp, LANES), big, F32)
    run = jnp.zeros((n_exp, 1), F32)
    for j in range(nch):
        inc = _dot(taken[:, j * cw:(j + 1) * cw].astype(BF16), upper)
        incs.append(inc)
        chunk_lo = jnp.where(lane == j, run, chunk_lo)
        run = run + inc[:, cw - 1:cw]
        chunk_hi = jnp.where(lane == j, run, chunk_hi)
    lc_sc[...] = jnp.zeros_like(lc_sc)
    slot_col = lax.broadcasted_iota(I32, (n_slots, 1), 0).astype(F32)
    lane_s = lax.broadcasted_iota(I32, (n_slots, LANES), 1)
    lane_sf = lane_s.astype(F32)
    toks = jnp.zeros((n_slots, LANES), F32)
    for e in range(n_exp):
        for j in range(nch):
            lc_sc[j:j + 1, :] = incs[j][e:e + 1, :]
        lo_e = chunk_lo[e:e + 1, :]
        hi_e = chunk_hi[e:e + 1, :]
        chunk = jnp.sum(jnp.where(hi_e <= slot_col, 1.0, 0.0), axis=1, keepdims=True)
        first = jnp.max(jnp.where(lo_e <= slot_col, lo_e, 0.0), axis=1, keepdims=True)
        rank = slot_col - first
        pick = jnp.where(lane_sf == chunk, 1.0, 0.0).astype(BF16)
        local = _dot(pick, lc_sc[...].astype(BF16))
        pos = jnp.sum(jnp.where(local <= rank, 1.0, 0.0), axis=1, keepdims=True)
        toks = jnp.where(lane_s == e, chunk * cw + pos, toks)
    tok_ref[...] = toks.astype(I32)


ROUTE_CHUNK = 256


def _route(aff_t, segments, scatter_tile, n_slots):
    bsz, n_exp, L = aff_t.shape
    assert L % ROUTE_CHUNK == 0 and L // ROUTE_CHUNK <= LANES and n_exp <= LANES
    return pl.pallas_call(
        functools.partial(_route_kernel, segments=segments, scatter_tile=scatter_tile, n_slots=n_slots),
        grid=(bsz,),
        in_specs=[pl.BlockSpec((None, n_exp, L), lambda b: (b, 0, 0))],
        out_specs=[pl.BlockSpec((None, n_exp, L), lambda b: (b, 0, 0)),
                   pl.BlockSpec((None, n_exp, LANES), lambda b: (b, 0, 0)),
                   pl.BlockSpec((None, n_slots, LANES), lambda b: (b, 0, 0))],
        out_shape=[jax.ShapeDtypeStruct((bsz, n_exp, L), I32),
                   jax.ShapeDtypeStruct((bsz, n_exp, LANES), I32),
                   jax.ShapeDtypeStruct((bsz, n_slots, LANES), I32)],
        scratch_shapes=[pltpu.VMEM((LANES, ROUTE_CHUNK), F32)],
        compiler_params=_cparams(("arbitrary",), 32),
        name="moe_route",
    )(aff_t)


GATHER_UNROLL = 8


def _gather_kernel(rows_ref, next_rows_ref, src_ref, o_ref, buf, sem, *, n_slots):
    step = pl.program_id(0) * pl.num_programs(1) + pl.program_id(1)
    last = pl.num_programs(0) * pl.num_programs(1) - 1
    cur = lax.rem(step, 2)

    def start_rows(table_ref, which):
        def issue(s, carry):
            pltpu.make_async_copy(src_ref.at[pl.ds(table_ref[0, s], 1), :], buf.at[which, pl.ds(s, 1), :],
                                  sem.at[which]).start()
            return carry
        lax.fori_loop(0, n_slots, issue, 0, unroll=GATHER_UNROLL)

    @pl.when(step == 0)
    def _first():
        start_rows(rows_ref, 0)

    @pl.when(step < last)
    def _ahead():
        start_rows(next_rows_ref, 1 - cur)

    pltpu.make_async_copy(src_ref.at[pl.ds(0, n_slots), :], buf.at[cur], sem.at[cur]).wait()
    o_ref[...] = buf[cur]


def _gather(rows, b_pack, n_slots):
    n_exp, bsz = rows.shape[:2]
    half = b_pack.shape[1]

    def next_step(e, b):
        nb = jnp.where(b + 1 == bsz, 0, b + 1)
        ne = jnp.minimum(jnp.where(b + 1 == bsz, e + 1, e), n_exp - 1)
        return ne, nb, 0, 0

    return pl.pallas_call(
        functools.partial(_gather_kernel, n_slots=n_slots),
        grid=(n_exp, bsz),
        in_specs=[
            pl.BlockSpec((None, None, 1, n_slots), lambda e, b: (e, b, 0, 0), memory_space=pltpu.SMEM),
            pl.BlockSpec((None, None, 1, n_slots), next_step, memory_space=pltpu.SMEM),
            pl.BlockSpec(memory_space=pl.ANY),
        ],
        out_specs=pl.BlockSpec((None, None, n_slots, half), lambda e, b: (e, b, 0, 0)),
        out_shape=jax.ShapeDtypeStruct((n_exp, bsz, n_slots, half), U32),
        scratch_shapes=[pltpu.VMEM((2, n_slots, half), U32), pltpu.SemaphoreType.DMA((2,))],
        compiler_params=_cparams(("arbitrary", "arbitrary"), 32),
        name="moe_gather",
    )(rows, rows, b_pack)


def _expert_up_kernel(x_ref, wg_ref, wu_ref, o_ref, xlo_sc, xhi_sc):
    half = x_ref.shape[1]

    @pl.when(pl.program_id(1) == 0)
    def _unpack():
        w = x_ref[...]
        xlo_sc[...] = pltpu.bitcast(lax.shift_left(w, jnp.uint32(16)), F32).astype(BF16)
        xhi_sc[...] = pltpu.bitcast(w & jnp.uint32(0xFFFF0000), F32).astype(BF16)

    xlo = xlo_sc[...]
    xhi = xhi_sc[...]
    hg = _dot(xlo, wg_ref[0:half, :].astype(BF16)) + _dot(xhi, wg_ref[half:, :].astype(BF16))
    hu = _dot(xlo, wu_ref[0:half, :].astype(BF16)) + _dot(xhi, wu_ref[half:, :].astype(BF16))
    o_ref[...] = (hg * _sigmoid(hg) * hu).astype(BF16)


def _expert_down_kernel(h_ref, wd_ref, o_ref):
    o_ref[...] = _dot(h_ref[...], wd_ref[...].astype(BF16)).astype(BF16)


EXPERT_UP_COLS = 256
EXPERT_DOWN_COLS = 512


def _experts(xs, w_gate, w_up, w_down, layer):
    n_exp, rows, half = xs.shape
    d = 2 * half
    ff = w_gate.shape[3]
    tf = EXPERT_UP_COLS
    hid = pl.pallas_call(
        _expert_up_kernel,
        grid=(n_exp, ff // tf),
        in_specs=[
            pl.BlockSpec((None, rows, half), lambda e, f: (e, 0, 0)),
            pl.BlockSpec((None, None, d, tf), lambda e, f: (layer, e, 0, f)),
            pl.BlockSpec((None, None, d, tf), lambda e, f: (layer, e, 0, f)),
        ],
        out_specs=pl.BlockSpec((None, rows, tf), lambda e, f: (e, 0, f)),
        out_shape=jax.ShapeDtypeStruct((n_exp, rows, ff), BF16),
        scratch_shapes=[pltpu.VMEM((rows, half), BF16), pltpu.VMEM((rows, half), BF16)],
        compiler_params=_cparams(("arbitrary", "arbitrary"), 48),
        name="moe_expert_up",
    )(xs, w_gate, w_up)
    tn = EXPERT_DOWN_COLS
    return pl.pallas_call(
        _expert_down_kernel,
        grid=(n_exp, d // tn),
        in_specs=[
            pl.BlockSpec((None, rows, ff), lambda e, n: (e, 0, 0)),
            pl.BlockSpec((None, None, ff, tn), lambda e, n: (layer, e, 0, n)),
        ],
        out_specs=pl.BlockSpec((None, rows, tn), lambda e, n: (e, 0, n)),
        out_shape=jax.ShapeDtypeStruct((n_exp, rows, d), BF16),
        compiler_params=_cparams(("arbitrary", "arbitrary"), 48),
        name="moe_expert_down",
    )(hid, w_down)


SCATTER_WINDOW = 256
SCATTER_EXPERTS = 2
SCATTER_TILES = 4


def _scatter_kernel(base_ref, keyt_ref, afft_ref, ys_ref, h_ref, gl_ref, gc_ref, gf_ref, o_ref,
                    *, n_slots, ctx_len, tm, n_tiles, final_norm):
    b = pl.program_id(0)
    t = pl.program_id(1)
    eg = pl.program_id(2)
    n_exp = keyt_ref.shape[1]
    win = min(SCATTER_WINDOW, n_slots)
    lane = lax.broadcasted_iota(I32, (tm, n_exp), 1)
    keyt = keyt_ref[...].astype(F32)
    afft = afft_ref[...]
    @pl.when(eg == 0)
    def _zero():
        o_ref[...] = jnp.zeros_like(o_ref)

    total = o_ref[...]
    cols = []
    for k in range(SCATTER_EXPERTS):
        e = eg * SCATTER_EXPERTS + k
        pick = lane == e
        kcol = jnp.sum(jnp.where(pick, keyt, 0.0), axis=1, keepdims=True)
        gcol = jnp.sum(jnp.where(pick, afft, 0.0), axis=1, keepdims=True)
        lo = base_ref[(b * (n_tiles + 1) + t) * n_exp + e]
        hi = base_ref[(b * (n_tiles + 1) + t + 1) * n_exp + e]
        start = jnp.minimum((lo // (2 * SUBLANES)) * (2 * SUBLANES), n_slots - win)
        start = pl.multiple_of(start, 2 * SUBLANES)
        slot = (start + lax.broadcasted_iota(I32, (1, win), 1)).astype(F32)
        weights = jnp.where(kcol == slot, gcol, 0.0).astype(BF16)
        part = _dot(weights, ys_ref[k, pl.ds(start, win), :])
        total = total + part
        cols.append((kcol, gcol, start, hi))
    o_ref[...] = total

    for k, (kcol, gcol, start, hi) in enumerate(cols):
        @pl.when(hi > start + win)
        def _beyond_window():
            slot = lax.broadcasted_iota(I32, (1, n_slots), 1)
            outside = jnp.logical_or(slot < start, slot >= start + win)
            hit = jnp.logical_and(kcol == slot.astype(F32), outside)
            weights = jnp.where(hit, gcol, 0.0).astype(BF16)
            o_ref[...] += _dot(weights, ys_ref[k])

    @pl.when(eg == pl.num_programs(2) - 1)
    def _store():
        row = t * tm + lax.broadcasted_iota(I32, (tm, 1), 0)
        gate = jnp.where(row < ctx_len, gc_ref[...], gl_ref[...])
        hn = h_ref[...] + gate * o_ref[...]
        if final_norm:
            hn = _rms(hn, gf_ref[...])
        o_ref[...] = hn


def _scatter(base, key_t, aff_tt, ys, h, mods3, gfinal, n_slots, ctx_len, final_norm):
    bsz, L, d = h.shape
    n_exp = key_t.shape[2]
    n_tiles = SCATTER_TILES
    tm = L // n_tiles
    assert n_exp % SCATTER_EXPERTS == 0 and (n_slots - min(SCATTER_WINDOW, n_slots)) % (2 * SUBLANES) == 0
    kern = functools.partial(_scatter_kernel, n_slots=n_slots, ctx_len=ctx_len, tm=tm, n_tiles=n_tiles,
                             final_norm=final_norm)
    grid_spec = pltpu.PrefetchScalarGridSpec(
        num_scalar_prefetch=1,
        grid=(bsz, n_tiles, n_exp // SCATTER_EXPERTS),
        in_specs=[
            pl.BlockSpec((None, tm, n_exp), lambda b, t, e, *_: (b, t, 0)),
            pl.BlockSpec((None, tm, n_exp), lambda b, t, e, *_: (b, t, 0)),
            pl.BlockSpec((SCATTER_EXPERTS, None, n_slots, d), lambda b, t, e, *_: (e, b, 0, 0)),
            pl.BlockSpec((None, tm, d), lambda b, t, e, *_: (b, t, 0), pipeline_mode=pl.Buffered(1)),
            pl.BlockSpec((None, 1, d), lambda b, t, e, *_: (b, 0, 5)),
            pl.BlockSpec((None, 1, d), lambda b, t, e, *_: (bsz, 0, 5)),
            pl.BlockSpec((1, d), lambda b, t, e, *_: (0, 0)),
        ],
        out_specs=pl.BlockSpec((None, tm, d), lambda b, t, e, *_: (b, t, 0)),
    )
    return pl.pallas_call(
        kern,
        grid_spec=grid_spec,
        out_shape=jax.ShapeDtypeStruct((bsz, L, d), F32),
        compiler_params=_cparams(("arbitrary", "arbitrary", "arbitrary"), 56),
        name="moe_scatter",
    )(base, key_t, aff_tt, ys, h, mods3, mods3, gfinal)


def _rope_tables(ctx_len, seq):
    pos = np.arange(seq)
    freqs = ROPE_THETA ** (-jnp.arange(0, ROT_AXIS, 2, dtype=F32) / ROT_AXIS)

    def cs(p):
        ang = jnp.asarray(p, F32)[:, None] * freqs[None, :]
        return jnp.cos(ang), jnp.sin(ang)

    cr, sr = cs(pos // GRID_W)
    cc, sc = cs(pos % GRID_W)
    cos_l = jnp.concatenate([cr, cr, cc, cc], axis=1)
    sin_l = jnp.concatenate([-sr, sr, -sc, sc], axis=1)
    cos_t = jnp.concatenate([jnp.ones((ctx_len, HEAD_DIM), F32), cos_l], axis=0)
    sin_t = jnp.concatenate([jnp.zeros((ctx_len, HEAD_DIM), F32), sin_l], axis=0)
    return cos_t, sin_t


def _moe(h, b_act, aff_t, mods3, w_gate, w_up, w_down, layer, gfinal, ctx_len, moe_ctx, final_norm):
    bsz, L, d = h.shape
    seq = L - ctx_len
    n_exp = aff_t.shape[1]
    cap_l = max(1, EC_FACTOR * seq // n_exp)
    cap_c = max(1, EC_FACTOR * ctx_len // n_exp) if moe_ctx else 0
    n_slots = cap_l + cap_c
    segments = ((0, ctx_len, cap_c, 0), (ctx_len, seq, cap_l, cap_c))
    key, base, tok = _route(aff_t, segments, L // SCATTER_TILES, n_slots)
    rows = tok[:, :, :n_exp].transpose(2, 0, 1) + (jnp.arange(bsz, dtype=I32) * L)[None, :, None]
    xs = _gather(rows.reshape(n_exp, bsz, 1, n_slots), b_act.reshape(bsz * L, d // 2), n_slots)
    ys = _experts(xs.reshape(n_exp, bsz * n_slots, d // 2), w_gate, w_up, w_down, layer)
    ys = ys.reshape(n_exp, bsz, n_slots, d)
    base_flat = base[:, :, :SCATTER_TILES + 1].transpose(0, 2, 1).reshape(-1)
    return _scatter(base_flat, key.transpose(0, 2, 1), aff_t.transpose(0, 2, 1), ys, h, mods3, gfinal,
                    n_slots, ctx_len, final_norm)


def kernel(x, c, ctx, c_ctx, mod_w, mod_b, norm1_g, norm2_g, ev_w_in, ev_q_norm, ev_k_norm, ev_conv_w, ev_conv_b, ev_w_out, od_w_in, od_conv_w, od_conv_b, od_wa, od_ba, od_wx, od_bx, od_lam, od_w_out, moe_router, moe_w_gate, moe_w_up, moe_w_down, final_norm_g):
    bsz, seq, d = x.shape
    ctx_len = ctx.shape[1]
    depth = mod_w.shape[0]
    assert bsz < SUBLANES and ctx_len % ROW_TILE == 0 and seq % (2 * KV_CHUNK) == 0
    attn_w = d // 2
    kv_w = attn_w // Q_PER_KV
    conv_w = d // 2
    lru_w = od_w_in.shape[2] // 2

    h = (ctx, x)
    cvec = jnp.concatenate([c, c_ctx[None, :], jnp.zeros((SUBLANES - bsz - 1, d), F32)], axis=0)
    mods = _mods(cvec, mod_w, mod_b)
    cos_t, sin_t = _rope_tables(ctx_len, seq)
    gfinal = final_norm_g.reshape(1, d)

    rows_ctx = ctx_len
    for i in range(depth):
        last = i == depth - 1
        mods3 = mods[i].reshape(SUBLANES, 1, 6 * d)
        g1 = norm1_g[i].reshape(1, d)
        g2 = norm2_g[i].reshape(1, d)
        wr = moe_router[i].T
        wr_hi = wr.astype(BF16)
        wr_lo = (wr - wr_hi.astype(F32)).astype(BF16)
        j = i // 2
        if i % 2 == 0:
            q, k, vt, gb, p = _even_in(h, mods3, g1, ev_w_in[j].astype(BF16),
                                      ev_q_norm[j].reshape(1, HEAD_DIM), ev_k_norm[j].reshape(1, HEAD_DIM),
                                      cos_t, sin_t, ctx_len, attn_w, kv_w, conv_w)
            att = _attention(q, k, vt, ctx_len)
            h, b_act, aff_t = _even_out(h, att, gb, p, ev_conv_w[j], ev_conv_b[j].reshape(1, conv_w),
                                        ev_w_out[j].astype(BF16), mods3, g2, wr_hi, wr_lo, ctx_len)
        else:
            u, gg, edge = _odd_in(h, mods3, g1, od_w_in[j].astype(BF16), od_conv_w[j],
                                  od_conv_b[j].reshape(1, lru_w), ctx_len, lru_w)
            y2 = _lru(u, edge, od_conv_w[j], od_wa[j].astype(BF16), od_wx[j].astype(BF16),
                      od_ba[j].reshape(2, 1, lru_w), od_bx[j].reshape(2, 1, lru_w),
                      od_lam[j].reshape(2, 1, lru_w), ctx_len)
            h, b_act, aff_t = _odd_out(h, y2, gg, od_w_out[j].astype(BF16), mods3, g2, wr_hi, wr_lo, ctx_len,
                                       latent_only=last)
            if last:
                rows_ctx = 0
        h = _moe(h, b_act, aff_t, mods3, moe_w_gate, moe_w_up, moe_w_down, i, gfinal,
                 rows_ctx, moe_ctx=not last, final_norm=last)
    return h[:, rows_ctx:, :]
```
